```python
import jax, jax.numpy as jnp
from jax import lax
import numpy as np

D_MODEL = 1024
BATCH = 4
SEQ = 8192
DEPTH = 2

CHUNK = 64
N_BRANCH = 4
BRANCH = D_MODEL // N_BRANCH
D_MIX = N_BRANCH * BRANCH
EPS = 1e-6

CONV_A_WIDTH = 3
CONV_A_GROUPS = 4
GLA_HEADS = 4
GLA_DK = BRANCH // 2
GLA_DV = BRANCH
GLA_GATE_RANK = 16
GLA_TAU = 16.0
POOL_WINDOWS = (2, 4, 8, 16)
POOL_GROUP = BRANCH // len(POOL_WINDOWS)
SSD_HEAD_DIM = 64
SSD_HEADS = BRANCH // SSD_HEAD_DIM
SSD_GROUPS = 2
SSD_STATE = 128
SSD_CONV = 4
SSD_XBC = BRANCH + 2 * SSD_GROUPS * SSD_STATE

PROJ_SPLITS = (BRANCH, BRANCH, BRANCH, BRANCH,
               GLA_DK, GLA_DK, GLA_DV, GLA_GATE_RANK, GLA_DV,
               BRANCH, BRANCH,
               BRANCH, SSD_XBC, SSD_HEADS)
D_PROJ = sum(PROJ_SPLITS)

kernel_name = 'hybrid_parallel_conv_gla_pool_ssd'


def _rmsnorm(x, w):
    xf = x.astype(jnp.float32)
    y = xf * lax.rsqrt(jnp.mean(xf * xf, axis=-1, keepdims=True) + EPS)
    return (y * w.astype(jnp.float32)).astype(x.dtype)


def _causal_depthwise_conv(u, w):
    k = w.shape[0]
    return lax.conv_general_dilated(
        u, w[:, None, :].astype(u.dtype), window_strides=(1,), padding=[(k - 1, 0)],
        dimension_numbers=('NWC', 'WIO', 'NWC'), feature_group_count=u.shape[-1])


def _chunk_states(decay, update):
    def step(s, inp):
        a_c, u_c = inp
        s = a_c * s + u_c
        return s, s
    init = jnp.zeros_like(update[:, 0])
    _, states = lax.scan(step, init, (jnp.moveaxis(decay, 1, 0), jnp.moveaxis(update, 1, 0)))
    return jnp.moveaxis(states, 0, 1)


def _conv_mixer(h, bg, cg, z, conv_w):
    y = bg * _causal_depthwise_conv(cg * h, conv_w)
    return y * jax.nn.silu(z)


def _gla_mixer(q, k, v, g_lr, z, gate_w, gate_b, norm_w):
    b, l, _ = q.shape
    nc = l // CHUNK
    dk = GLA_DK // GLA_HEADS
    dv = GLA_DV // GLA_HEADS
    f32 = jnp.float32
    qc = q.astype(f32).reshape(b, nc, CHUNK, GLA_HEADS, dk) * (dk ** -0.5)
    kc = k.astype(f32).reshape(b, nc, CHUNK, GLA_HEADS, dk)
    vc = v.astype(f32).reshape(b, nc, CHUNK, GLA_HEADS, dv)
    log_a = jax.nn.log_sigmoid((g_lr @ gate_w + gate_b).astype(f32)) / GLA_TAU
    log_a = log_a.reshape(b, nc, CHUNK, GLA_HEADS, dk)
    cum = jnp.cumsum(log_a, axis=2)
    total = cum[:, :, -1]
    k_dec = kc * jnp.exp(total[:, :, None] - cum)
    upd = jnp.einsum('bnchk,bnchv->bnhkv', k_dec, vc)
    states = _chunk_states(jnp.exp(total)[..., None], upd)
    o = jnp.einsum('bnchk,bnhkv->bnchv', qc, states)
    o = _rmsnorm(o, norm_w).reshape(b, l, GLA_DV)
    return (o * jax.nn.silu(z.astype(f32))).astype(q.dtype)


def _pool_mixer(u, z, pool_w, pool_scale):
    b, l, _ = u.shape
    f32 = jnp.float32
    uf = u.astype(f32)
    cs = jnp.pad(jnp.cumsum(uf, axis=1), ((0, 0), (1, 0), (0, 0)))
    pos = jnp.arange(l)
    outs = []
    for gi, win in enumerate(POOL_WINDOWS):
        sl = slice(gi * POOL_GROUP, (gi + 1) * POOL_GROUP)
        cg = cs[:, :, sl]
        prev = jnp.pad(cg, ((0, 0), (win - 1, 0), (0, 0)))[:, :l]
        cnt = jnp.minimum(pos + 1, win).astype(f32)[None, :, None]
        outs.append((cg[:, 1:] - prev) / cnt - uf[:, :, sl])
    pooled = jnp.stack(outs, axis=2)
    mixed = jnp.einsum('blgc,gcd->blgd', pooled, pool_w.astype(f32)).reshape(b, l, BRANCH)
    return (pool_scale.astype(f32) * mixed * jax.nn.silu(z.astype(f32))).astype(u.dtype)


def _ssd_mixer(xbc, dt, z, conv_w, conv_b, dt_bias, a_log, d_skip, norm_w):
    b, l, _ = xbc.shape
    nc = l // CHUNK
    hpg = SSD_HEADS // SSD_GROUPS
    f32 = jnp.float32
    xbc = jax.nn.silu(_causal_depthwise_conv(xbc, conv_w) + conv_b).astype(f32)
    xs, bm, cm = jnp.split(xbc, [BRANCH, BRANCH + SSD_GROUPS * SSD_STATE], axis=-1)
    xs = xs.reshape(b, nc, CHUNK, SSD_GROUPS, hpg, SSD_HEAD_DIM)
    bm = bm.reshape(b, nc, CHUNK, SSD_GROUPS, SSD_STATE)
    cm = cm.reshape(b, nc, CHUNK, SSD_GROUPS, SSD_STATE)
    dt = jax.nn.softplus(dt.astype(f32) + dt_bias.astype(f32)).reshape(b, nc, CHUNK, SSD_GROUPS, hpg)
    a = -jnp.exp(a_log.astype(f32)).reshape(SSD_GROUPS, hpg)
    cum = jnp.cumsum(dt * a, axis=2)
    total = cum[:, :, -1]
    w = jnp.exp(total[:, :, None] - cum) * dt
    upd = jnp.einsum('bncgs,bncgh,bncghp->bnghps', bm, w, xs)
    states = _chunk_states(jnp.exp(total)[..., None, None], upd)
    y = jnp.einsum('bncgs,bnghps->bncghp', cm, states)
    y = y + d_skip.astype(f32).reshape(SSD_GROUPS, hpg)[:, :, None] * xs
    y = y.reshape(b, l, BRANCH)
    y = _rmsnorm(y * jax.nn.silu(z.astype(f32)), norm_w)
    return y.astype(z.dtype)


def setup_inputs(seed: int = 0) -> dict:
    key = jax.random.key(seed)
    ks = jax.random.split(key, 20)
    f32 = jnp.float32
    nrm = lambda k, shape, s: jax.random.normal(k, shape, f32) * s
    x = jax.random.normal(ks[0], (BATCH, SEQ, D_MODEL), f32)
    norm_w = 1.0 + nrm(ks[1], (DEPTH, D_MODEL), 0.02)
    w_in = nrm(ks[2], (DEPTH, D_MODEL, D_PROJ), D_MODEL ** -0.5)
    conv_a_w = nrm(ks[3], (DEPTH, CONV_A_WIDTH, BRANCH), CONV_A_WIDTH ** -0.5)
    gla_gate_w = nrm(ks[4], (DEPTH, GLA_GATE_RANK, GLA_DK), GLA_GATE_RANK ** -0.5)
    gla_gate_b = nrm(ks[5], (DEPTH, GLA_DK), 0.1)
    gla_norm_w = 1.0 + nrm(ks[6], (DEPTH, GLA_DV // GLA_HEADS), 0.02)
    pool_w = nrm(ks[7], (DEPTH, len(POOL_WINDOWS), POOL_GROUP, POOL_GROUP), POOL_GROUP ** -0.5)
    pool_scale = 1.0 + nrm(ks[8], (DEPTH, BRANCH), 0.1)
    ssd_conv_w = nrm(ks[9], (DEPTH, SSD_CONV, SSD_XBC), SSD_CONV ** -0.5)
    ssd_conv_b = nrm(ks[10], (DEPTH, SSD_XBC), 0.01)
    dt0 = jnp.exp(jax.random.uniform(ks[11], (DEPTH, SSD_HEADS), f32, np.log(1e-3), np.log(1e-1)))
    ssd_dt_bias = dt0 + jnp.log(-jnp.expm1(-dt0))
    ssd_a_log = jnp.log(jax.random.uniform(ks[12], (DEPTH, SSD_HEADS), f32, 1.0, 16.0))
    ssd_d = 1.0 + nrm(ks[13], (DEPTH, SSD_HEADS), 0.1)
    ssd_norm_w = 1.0 + nrm(ks[14], (DEPTH, BRANCH), 0.02)
    w_out = nrm(ks[15], (DEPTH, D_MIX, D_MODEL), D_MIX ** -0.5)
    final_norm_w = 1.0 + nrm(ks[16], (D_MODEL,), 0.02)
    return {'x': x, 'norm_w': norm_w, 'w_in': w_in, 'conv_a_w': conv_a_w,
            'gla_gate_w': gla_gate_w, 'gla_gate_b': gla_gate_b, 'gla_norm_w': gla_norm_w,
            'pool_w': pool_w, 'pool_scale': pool_scale,
            'ssd_conv_w': ssd_conv_w, 'ssd_conv_b': ssd_conv_b, 'ssd_dt_bias': ssd_dt_bias,
            'ssd_a_log': ssd_a_log, 'ssd_d': ssd_d, 'ssd_norm_w': ssd_norm_w,
            'w_out': w_out, 'final_norm_w': final_norm_w}


def reference(x, norm_w, w_in, conv_a_w, gla_gate_w, gla_gate_b, gla_norm_w, pool_w, pool_scale,
              ssd_conv_w, ssd_conv_b, ssd_dt_bias, ssd_a_log, ssd_d, ssd_norm_w, w_out, final_norm_w):
    split_idx = [int(i) for i in np.cumsum(PROJ_SPLITS)[:-1]]
    for layer in range(DEPTH):
        h = _rmsnorm(x, norm_w[layer])
        proj = h @ w_in[layer]
        (a_h, a_b, a_c, a_z, g_q, g_k, g_v, g_lr, g_z, p_u, p_z, s_z, s_xbc, s_dt) = jnp.split(proj, split_idx, axis=-1)
        y_a = _conv_mixer(a_h, a_b, a_c, a_z, conv_a_w[layer])
        y_b = _gla_mixer(g_q, g_k, g_v, g_lr, g_z, gla_gate_w[layer], gla_gate_b[layer], gla_norm_w[layer])
        y_c = _pool_mixer(p_u, p_z, pool_w[layer], pool_scale[layer])
        y_d = _ssd_mixer(s_xbc, s_dt, s_z, ssd_conv_w[layer], ssd_conv_b[layer], ssd_dt_bias[layer],
                         ssd_a_log[layer], ssd_d[layer], ssd_norm_w[layer])
        mix = jnp.concatenate([y_a, y_b, y_c, y_d], axis=-1)
        x = x + mix @ w_out[layer]
    return _rmsnorm(x, final_norm_w)
```

```python
import functools

import jax
import jax.numpy as jnp
from jax import lax
from jax.experimental import pallas as pl
from jax.experimental.pallas import tpu as pltpu

D_MODEL = 1024
CHUNK = 64
BRANCH = 256
EPS = 1e-6
CONV_A_WIDTH = 3
GLA_HEADS = 4
GLA_DK = 128
GLA_DV = 256
GLA_GATE_RANK = 16
GLA_TAU = 16.0
POOL_WINDOWS = (2, 4, 8, 16)
POOL_GROUP = 64
SSD_HEAD_DIM = 64
SSD_HEADS = 4
SSD_GROUPS = 2
SSD_STATE = 128
SSD_CONV = 4
SSD_XBC = 768

C_AH, C_AB, C_AC, C_AZ = 0, 256, 512, 768
C_Q, C_K, C_V, C_GZ = 1024, 1152, 1280, 1536
C_U, C_PZ = 1792, 2048
C_SZ, C_XBC = 2304, 2560
C_MISC = 3328
D_PROJ_PAD = 3456
DT_LANE = GLA_GATE_RANK

LANE = 128
SUBLANE = 8
TILE = 256
VMEM_LIMIT_BYTES = 48 * 1024 * 1024

R_NORM, R_FINAL, R_MISC, R_B256, R_SCB, R_CAW, R_SCW = 0, 1, 2, 3, 4, 5, 8
VEC_ROWS = 16

F32 = jnp.float32
BF16 = jnp.bfloat16


def _silu(x):
    return x * jax.nn.sigmoid(x)


def _log1pexp_neg_abs(x):
    return jnp.log1p(jnp.exp(-jnp.abs(x)))


def _shift_rows(cur, prev, j):
    p = prev.shape[0]
    ext = jnp.concatenate([prev, cur], axis=0)
    return pltpu.roll(ext, j, 0)[p:]


def _split3(x):
    hi = x.astype(BF16)
    r1 = x - hi.astype(F32)
    mid = r1.astype(BF16)
    lo = (r1 - mid.astype(F32)).astype(BF16)
    return hi, mid, lo


def _dot(a, b):
    return jnp.dot(a, b, preferred_element_type=F32)


def _dot_tn(a, b):
    return lax.dot_general(a, b, (((0,), (0,)), ((), ())), preferred_element_type=F32)


def _dot_nt(a, b):
    return lax.dot_general(a, b, (((1,), (1,)), ((), ())), preferred_element_type=F32)


def _pair_bcast(x, lane0, rows):
    lane = lax.broadcasted_iota(jnp.int32, (rows, LANE), 1)
    b0 = jnp.broadcast_to(x[:, lane0:lane0 + 1], (rows, LANE))
    b1 = jnp.broadcast_to(x[:, lane0 + 1:lane0 + 2], (rows, LANE))
    return jnp.where(lane < SSD_HEAD_DIM, b0, b1)


def _layer_kernel(x_ref, w_in_ref, w_out_ref, gate_w_ref, pool_w_ref, vec_ref,
                  o_ref, hist_a, hist_x, hist_u, gla_st, ssd_st, *, final_norm):
    t = x_ref.shape[0]
    n_chunks = t // CHUNK
    i = pl.program_id(1)

    @pl.when(i == 0)
    def _():
        hist_a[...] = jnp.zeros_like(hist_a)
        hist_x[...] = jnp.zeros_like(hist_x)
        hist_u[...] = jnp.zeros_like(hist_u)
        gla_st[...] = jnp.zeros_like(gla_st)
        ssd_st[...] = jnp.zeros_like(ssd_st)

    x = x_ref[...]
    ms = jnp.mean(x * x, axis=-1, keepdims=True)
    h = (x * lax.rsqrt(ms + EPS) * vec_ref[R_NORM:R_NORM + 1, :]).astype(BF16)
    proj = _dot(h, w_in_ref[...])

    c = proj[:, C_AC:C_AC + BRANCH] * proj[:, C_AH:C_AH + BRANCH]
    prev_a = hist_a[...]
    conv = c * vec_ref[R_CAW + 2:R_CAW + 3, 0:BRANCH]
    conv = conv + _shift_rows(c, prev_a, 1) * vec_ref[R_CAW + 1:R_CAW + 2, 0:BRANCH]
    conv = conv + _shift_rows(c, prev_a, 2) * vec_ref[R_CAW:R_CAW + 1, 0:BRANCH]
    hist_a[...] = c[t - SUBLANE:, :]
    y_a = proj[:, C_AB:C_AB + BRANCH] * conv * _silu(proj[:, C_AZ:C_AZ + BRANCH])

    u = proj[:, C_U:C_U + BRANCH]
    hp = hist_u.shape[0]
    lane_u = lax.broadcasted_iota(jnp.int32, (1, BRANCH), 1)
    grp = jnp.right_shift(lane_u, 6)
    s = jnp.concatenate([hist_u[...], u], axis=0)
    s = s + pltpu.roll(s, 1, 0)
    s = s + jnp.where(grp >= 1, pltpu.roll(s, 2, 0), 0.0)
    s = s + jnp.where(grp >= 2, pltpu.roll(s, 4, 0), 0.0)
    s = s + jnp.where(grp >= 3, pltpu.roll(s, 8, 0), 0.0)
    win = s[hp:, :]
    hist_u[...] = u[t - hp:, :]
    window = jnp.left_shift(2, grp)
    pos = i * t + lax.broadcasted_iota(jnp.int32, (t, BRANCH), 0)
    cnt = jnp.minimum(pos + 1, window).astype(F32)
    pooled = win / cnt - u
    mixed = _dot(pooled.astype(BF16), pool_w_ref[...])
    y_c = vec_ref[R_B256:R_B256 + 1, 256:512] * mixed * _silu(proj[:, C_PZ:C_PZ + BRANCH])

    misc = proj[:, C_MISC:C_MISC + LANE]
    lane = lax.broadcasted_iota(jnp.int32, (1, LANE), 1)
    dt_mask = (lane >= DT_LANE) & (lane < DT_LANE + SSD_HEADS)
    pre = _dot(misc.astype(BF16), gate_w_ref[...]) + vec_ref[R_MISC:R_MISC + 1, 0:LANE]
    log_a = (jnp.minimum(pre, 0.0) - _log1pexp_neg_abs(pre)) * (1.0 / GLA_TAU)
    dt_pre = misc + vec_ref[R_MISC:R_MISC + 1, LANE:2 * LANE]
    dt = jnp.maximum(dt_pre, 0.0) + _log1pexp_neg_abs(dt_pre)
    a_vec = jnp.where(dt_mask, -jnp.exp(vec_ref[R_MISC:R_MISC + 1, 2 * LANE:3 * LANE]), 0.0)
    d_a = jnp.where(dt_mask, dt * a_vec, 0.0)
    decay_in = jnp.concatenate([log_a, d_a], axis=1)

    rr = lax.broadcasted_iota(jnp.int32, (2 * CHUNK, CHUNK), 0)
    cc = lax.broadcasted_iota(jnp.int32, (2 * CHUNK, CHUNK), 1)
    suffix_op = jnp.where((cc > rr) | (rr >= CHUNK), 1.0, 0.0).astype(BF16)

    xbc_raw = proj[:, C_XBC:C_XBC + SSD_XBC]
    prev_x = hist_x[...]
    xc = xbc_raw * vec_ref[R_SCW + 3:R_SCW + 4, 0:SSD_XBC] + vec_ref[R_SCB:R_SCB + 1, 0:SSD_XBC]
    for j in (1, 2, 3):
        xc = xc + _shift_rows(xbc_raw, prev_x, j) * vec_ref[R_SCW + 3 - j:R_SCW + 4 - j, 0:SSD_XBC]
    hist_x[...] = xbc_raw[t - SUBLANE:, :]
    xbc = _silu(xc)
    xs = xbc[:, 0:BRANCH]
    bm = xbc[:, BRANCH:BRANCH + SSD_GROUPS * SSD_STATE].astype(BF16)
    cm = xbc[:, BRANCH + SSD_GROUPS * SSD_STATE:].astype(BF16)

    q = (proj[:, C_Q:C_Q + GLA_DK] * ((GLA_DK // GLA_HEADS) ** -0.5)).astype(BF16)
    k = proj[:, C_K:C_K + GLA_DK]
    v = proj[:, C_V:C_V + GLA_DV].astype(BF16)

    vr = jnp.right_shift(lax.broadcasted_iota(jnp.int32, (GLA_DV, GLA_DK), 0), 6)
    kr = jnp.right_shift(lax.broadcasted_iota(jnp.int32, (GLA_DV, GLA_DK), 1), 5)
    head_mask = vr == kr

    g_state = gla_st[...]
    s_state = [ssd_st[g] for g in range(SSD_GROUPS)]
    o_parts, y_parts = [], []
    for ci in range(n_chunks):
        r0 = ci * CHUNK
        sl = slice(r0, r0 + CHUNK)
        parts = _split3(decay_in[sl, :])
        sums = _dot(suffix_op, parts[0]) + _dot(suffix_op, parts[1]) + _dot(suffix_op, parts[2])
        rev = sums[0:CHUNK, :]
        tot = sums[CHUNK:CHUNK + SUBLANE, :]

        k_dec = (k[sl, :] * jnp.exp(rev[:, 0:GLA_DK])).astype(BF16)
        upd = jnp.where(head_mask, _dot_tn(v[sl, :], k_dec), 0.0)
        g_dec = jnp.exp(tot[0:1, 0:GLA_DK])
        g_state = g_state * g_dec + upd
        o_parts.append(_dot_nt(q[sl, :], g_state.astype(BF16)))

        w_tok = jnp.exp(rev[:, LANE:2 * LANE]) * dt[sl, :]
        s_dec = jnp.exp(tot[:, LANE:2 * LANE])
        y_g = []
        for g in range(SSD_GROUPS):
            hs = slice(g * LANE, (g + 1) * LANE)
            w_exp = _pair_bcast(w_tok, DT_LANE + 2 * g, CHUNK)
            wx = (xs[sl, hs] * w_exp).astype(BF16)
            upd_s = _dot_tn(bm[sl, hs], wx)
            dec = _pair_bcast(s_dec, DT_LANE + 2 * g, SUBLANE)[0:1, :]
            s_state[g] = s_state[g] * dec + upd_s
            y_g.append(_dot(cm[sl, hs], s_state[g].astype(BF16)))
        y_parts.append(jnp.concatenate(y_g, axis=1))

    gla_st[...] = g_state
    for g in range(SSD_GROUPS):
        ssd_st[g] = s_state[g]

    o = jnp.concatenate(o_parts, axis=0)
    hr = jnp.right_shift(lax.broadcasted_iota(jnp.int32, (GLA_DV, GLA_DV), 0), 6)
    hc = jnp.right_shift(lax.broadcasted_iota(jnp.int32, (GLA_DV, GLA_DV), 1), 6)
    head_mean = jnp.where(hr == hc, 1.0 / (GLA_DV // GLA_HEADS), 0.0).astype(BF16)
    o2 = o * o
    o2_hi = o2.astype(BF16)
    o2_lo = (o2 - o2_hi.astype(F32)).astype(BF16)
    o_ms = _dot(o2_hi, head_mean) + _dot(o2_lo, head_mean)
    y_b = (o * lax.rsqrt(o_ms + EPS) * vec_ref[R_B256:R_B256 + 1, 0:256]
           * _silu(proj[:, C_GZ:C_GZ + BRANCH]))

    y = jnp.concatenate(y_parts, axis=0) + vec_ref[R_B256:R_B256 + 1, 512:768] * xs
    y = y * _silu(proj[:, C_SZ:C_SZ + BRANCH])
    y_ms = jnp.mean(y * y, axis=-1, keepdims=True)
    y_d = y * lax.rsqrt(y_ms + EPS) * vec_ref[R_B256:R_B256 + 1, 768:1024]

    mix = jnp.concatenate([y_a, y_b, y_c, y_d], axis=1).astype(BF16)
    out = x + _dot(mix, w_out_ref[...])
    if final_norm:
        oms = jnp.mean(out * out, axis=-1, keepdims=True)
        out = out * lax.rsqrt(oms + EPS) * vec_ref[R_FINAL:R_FINAL + 1, :]
    o_ref[...] = out


def _layer_call(x2d, w_in, w_out, gate_w, pool_w, vecs, *, batch, final_norm):
    n_tok = x2d.shape[0]
    tiles = n_tok // batch // TILE
    const = lambda b, i: (0, 0)
    return pl.pallas_call(
        functools.partial(_layer_kernel, final_norm=final_norm),
        out_shape=jax.ShapeDtypeStruct(x2d.shape, F32),
        grid=(batch, tiles),
        in_specs=[
            pl.BlockSpec((TILE, D_MODEL), lambda b, i: (b * tiles + i, 0)),
            pl.BlockSpec(w_in.shape, const),
            pl.BlockSpec(w_out.shape, const),
            pl.BlockSpec(gate_w.shape, const),
            pl.BlockSpec(pool_w.shape, const),
            pl.BlockSpec(vecs.shape, const),
        ],
        out_specs=pl.BlockSpec((TILE, D_MODEL), lambda b, i: (b * tiles + i, 0)),
        scratch_shapes=[
            pltpu.VMEM((SUBLANE, BRANCH), F32),
            pltpu.VMEM((SUBLANE, SSD_XBC), F32),
            pltpu.VMEM((2 * SUBLANE, BRANCH), F32),
            pltpu.VMEM((GLA_DV, GLA_DK), F32),
            pltpu.VMEM((SSD_GROUPS, SSD_STATE, LANE), F32),
        ],
        compiler_params=pltpu.CompilerParams(
            dimension_semantics=("arbitrary", "arbitrary"),
            vmem_limit_bytes=VMEM_LIMIT_BYTES),
    )(x2d, w_in, w_out, gate_w, pool_w, vecs)


def _pack_w_in(w):
    a, gq, gk, gv, glr, gz, pu, pz, sz, xbc, dt = (
        w[:, 0:1024], w[:, 1024:1152], w[:, 1152:1280], w[:, 1280:1536], w[:, 1536:1552],
        w[:, 1552:1808], w[:, 1808:2064], w[:, 2064:2320], w[:, 2320:2576], w[:, 2576:3344],
        w[:, 3344:3348])
    pad = jnp.zeros((w.shape[0], LANE - GLA_GATE_RANK - SSD_HEADS), w.dtype)
    return jnp.concatenate([a, gq, gk, gv, gz, pu, pz, sz, xbc, glr, dt, pad], axis=1).astype(BF16)


def _row(vals, width=D_MODEL):
    v = jnp.concatenate([jnp.ravel(p).astype(F32) for p in vals])
    return jnp.pad(v, (0, width - v.shape[0]))[None, :]


def _pack_vecs(layer, norm_w, gla_gate_b, gla_norm_w, pool_scale, conv_a_w, ssd_conv_w, ssd_conv_b,
               ssd_dt_bias, ssd_a_log, ssd_d, ssd_norm_w, final_norm_w):
    lane_pad = jnp.zeros((DT_LANE,), F32)
    tail_pad = jnp.zeros((LANE - DT_LANE - SSD_HEADS,), F32)
    rows = [
        _row([norm_w[layer]]),
        _row([final_norm_w]),
        _row([gla_gate_b[layer], lane_pad, ssd_dt_bias[layer], tail_pad,
              lane_pad, ssd_a_log[layer], tail_pad]),
        _row([jnp.tile(gla_norm_w[layer], GLA_HEADS), pool_scale[layer],
              jnp.repeat(ssd_d[layer], SSD_HEAD_DIM), ssd_norm_w[layer]]),
        _row([ssd_conv_b[layer]]),
    ]
    rows += [_row([conv_a_w[layer, r]]) for r in range(CONV_A_WIDTH)]
    rows += [_row([ssd_conv_w[layer, r]]) for r in range(SSD_CONV)]
    rows.append(jnp.zeros((VEC_ROWS - len(rows), D_MODEL), F32))
    return jnp.concatenate(rows, axis=0)


def _block_diag(blocks):
    n = blocks.shape[0]
    rows = []
    for g in range(n):
        rows.append(jnp.concatenate(
            [blocks[g] if j == g else jnp.zeros_like(blocks[g]) for j in range(n)], axis=1))
    return jnp.concatenate(rows, axis=0)


def kernel(x, norm_w, w_in, conv_a_w, gla_gate_w, gla_gate_b, gla_norm_w, pool_w, pool_scale,
           ssd_conv_w, ssd_conv_b, ssd_dt_bias, ssd_a_log, ssd_d, ssd_norm_w, w_out, final_norm_w):
    batch, seq, d_model = x.shape
    depth = w_in.shape[0]
    assert d_model == D_MODEL and seq % TILE == 0 and TILE % CHUNK == 0
    h = x.reshape(batch * seq, d_model)
    for layer in range(depth):
        gate_w = jnp.zeros((LANE, GLA_DK), F32).at[0:GLA_GATE_RANK].set(gla_gate_w[layer]).astype(BF16)
        vecs = _pack_vecs(layer, norm_w, gla_gate_b, gla_norm_w, pool_scale, conv_a_w, ssd_conv_w,
                          ssd_conv_b, ssd_dt_bias, ssd_a_log, ssd_d, ssd_norm_w, final_norm_w)
        h = _layer_call(h, _pack_w_in(w_in[layer]), w_out[layer].astype(BF16), gate_w,
                        _block_diag(pool_w[layer]).astype(BF16), vecs,
                        batch=batch, final_norm=(layer == depth - 1))
    return h.reshape(batch, seq, d_model)
```

```python
import functools

import jax
import jax.numpy as jnp
from jax import lax
from jax.experimental import pallas as pl
from jax.experimental.pallas import tpu as pltpu

D_MODEL = 1024
CHUNK = 64
BRANCH = 256
EPS = 1e-6
CONV_A_WIDTH = 3
GLA_HEADS = 4
GLA_DK = 128
GLA_DV = 256
GLA_GATE_RANK = 16
GLA_TAU = 16.0
POOL_WINDOWS = (2, 4, 8, 16)
POOL_GROUP = 64
SSD_HEAD_DIM = 64
SSD_HEADS = 4
SSD_GROUPS = 2
SSD_STATE = 128
SSD_CONV = 4
SSD_XBC = 768

C_AH, C_AB, C_AC, C_AZ = 0, 256, 512, 768
C_Q, C_K, C_V, C_GZ = 1024, 1152, 1280, 1536
C_U, C_PZ = 1792, 2048
C_SZ, C_XBC = 2304, 2560
C_MISC = 3328
D_PROJ_PAD = 3456
DT_LANE = GLA_GATE_RANK

LANE = 128
SUBLANE = 8
TILE = 256
VMEM_LIMIT_BYTES = 48 * 1024 * 1024

R_NORM, R_FINAL, R_MISC, R_B256, R_SCB, R_CAW, R_SCW = 0, 1, 2, 3, 4, 5, 8
VEC_ROWS = 16

F32 = jnp.float32
BF16 = jnp.bfloat16


def _silu(x):
    return x * jax.nn.sigmoid(x)


def _log1pexp_neg_abs(x):
    return jnp.log1p(jnp.exp(-jnp.abs(x)))


def _shift_rows(cur, prev, j):
    p = prev.shape[0]
    ext = jnp.concatenate([prev, cur], axis=0)
    return pltpu.roll(ext, j, 0)[p:]


def _split3(x):
    hi = x.astype(BF16)
    r1 = x - hi.astype(F32)
    mid = r1.astype(BF16)
    lo = (r1 - mid.astype(F32)).astype(BF16)
    return hi, mid, lo


def _dot(a, b):
    return jnp.dot(a, b, preferred_element_type=F32)


def _dot_tn(a, b):
    return lax.dot_general(a, b, (((0,), (0,)), ((), ())), preferred_element_type=F32)


def _dot_nt(a, b):
    return lax.dot_general(a, b, (((1,), (1,)), ((), ())), preferred_element_type=F32)


def _pair_bcast(x, lane0, rows):
    lane = lax.broadcasted_iota(jnp.int32, (rows, LANE), 1)
    b0 = jnp.broadcast_to(x[:, lane0:lane0 + 1], (rows, LANE))
    b1 = jnp.broadcast_to(x[:, lane0 + 1:lane0 + 2], (rows, LANE))
    return jnp.where(lane < SSD_HEAD_DIM, b0, b1)


def _layer_kernel(x_ref, w_in_ref, w_out_ref, gate_w_ref, pool_w_ref, vec_ref,
                  o_ref, hist_a, hist_x, hist_u, gla_st, ssd_st, *, final_norm):
    t = x_ref.shape[0]
    n_chunks = t // CHUNK
    i = pl.program_id(1)

    @pl.when(i == 0)
    def _():
        hist_a[...] = jnp.zeros_like(hist_a)
        hist_x[...] = jnp.zeros_like(hist_x)
        hist_u[...] = jnp.zeros_like(hist_u)
        gla_st[...] = jnp.zeros_like(gla_st)
        ssd_st[...] = jnp.zeros_like(ssd_st)

    x = x_ref[...]
    ms = jnp.mean(x * x, axis=-1, keepdims=True)
    h = (x * lax.rsqrt(ms + EPS) * vec_ref[R_NORM:R_NORM + 1, :]).astype(BF16)
    proj = _dot_nt(h, w_in_ref[...])

    c = proj[:, C_AC:C_AC + BRANCH] * proj[:, C_AH:C_AH + BRANCH]
    prev_a = hist_a[...]
    conv = c * vec_ref[R_CAW + 2:R_CAW + 3, 0:BRANCH]
    conv = conv + _shift_rows(c, prev_a, 1) * vec_ref[R_CAW + 1:R_CAW + 2, 0:BRANCH]
    conv = conv + _shift_rows(c, prev_a, 2) * vec_ref[R_CAW:R_CAW + 1, 0:BRANCH]
    hist_a[...] = c[t - SUBLANE:, :]
    y_a = proj[:, C_AB:C_AB + BRANCH] * conv * _silu(proj[:, C_AZ:C_AZ + BRANCH])

    u = proj[:, C_U:C_U + BRANCH]
    hp = hist_u.shape[0]
    lane_u = lax.broadcasted_iota(jnp.int32, (1, BRANCH), 1)
    grp = jnp.right_shift(lane_u, 6)
    s = jnp.concatenate([hist_u[...], u], axis=0)
    s = s + pltpu.roll(s, 1, 0)
    s = s + jnp.where(grp >= 1, pltpu.roll(s, 2, 0), 0.0)
    s = s + jnp.where(grp >= 2, pltpu.roll(s, 4, 0), 0.0)
    s = s + jnp.where(grp >= 3, pltpu.roll(s, 8, 0), 0.0)
    win = s[hp:, :]
    hist_u[...] = u[t - hp:, :]
    window = jnp.left_shift(2, grp)
    pos = i * t + lax.broadcasted_iota(jnp.int32, (t, BRANCH), 0)
    cnt = jnp.minimum(pos + 1, window).astype(F32)
    pooled = win / cnt - u
    mixed = _dot(pooled.astype(BF16), pool_w_ref[...])
    y_c = vec_ref[R_B256:R_B256 + 1, 256:512] * mixed * _silu(proj[:, C_PZ:C_PZ + BRANCH])

    misc = proj[:, C_MISC:C_MISC + LANE]
    lane = lax.broadcasted_iota(jnp.int32, (1, LANE), 1)
    dt_mask = (lane >= DT_LANE) & (lane < DT_LANE + SSD_HEADS)
    pre = _dot(misc.astype(BF16), gate_w_ref[...]) + vec_ref[R_MISC:R_MISC + 1, 0:LANE]
    log_a = (jnp.minimum(pre, 0.0) - _log1pexp_neg_abs(pre)) * (1.0 / GLA_TAU)
    dt_pre = misc + vec_ref[R_MISC:R_MISC + 1, LANE:2 * LANE]
    dt = jnp.maximum(dt_pre, 0.0) + _log1pexp_neg_abs(dt_pre)
    a_vec = jnp.where(dt_mask, -jnp.exp(vec_ref[R_MISC:R_MISC + 1, 2 * LANE:3 * LANE]), 0.0)
    d_a = jnp.where(dt_mask, dt * a_vec, 0.0)
    decay_in = jnp.concatenate([log_a, d_a], axis=1)

    rr = lax.broadcasted_iota(jnp.int32, (2 * CHUNK, CHUNK), 0)
    cc = lax.broadcasted_iota(jnp.int32, (2 * CHUNK, CHUNK), 1)
    suffix_op = jnp.where((cc > rr) | (rr >= CHUNK), 1.0, 0.0).astype(BF16)

    xbc_raw = proj[:, C_XBC:C_XBC + SSD_XBC]
    prev_x = hist_x[...]
    xc = xbc_raw * vec_ref[R_SCW + 3:R_SCW + 4, 0:SSD_XBC] + vec_ref[R_SCB:R_SCB + 1, 0:SSD_XBC]
    for j in (1, 2, 3):
        xc = xc + _shift_rows(xbc_raw, prev_x, j) * vec_ref[R_SCW + 3 - j:R_SCW + 4 - j, 0:SSD_XBC]
    hist_x[...] = xbc_raw[t - SUBLANE:, :]
    xbc = _silu(xc)
    xs = xbc[:, 0:BRANCH]
    bm = xbc[:, BRANCH:BRANCH + SSD_GROUPS * SSD_STATE].astype(BF16)
    cm = xbc[:, BRANCH + SSD_GROUPS * SSD_STATE:].astype(BF16)

    q = (proj[:, C_Q:C_Q + GLA_DK] * ((GLA_DK // GLA_HEADS) ** -0.5)).astype(BF16)
    k = proj[:, C_K:C_K + GLA_DK]
    v = proj[:, C_V:C_V + GLA_DV].astype(BF16)

    vr = jnp.right_shift(lax.broadcasted_iota(jnp.int32, (GLA_DV, GLA_DK), 0), 6)
    kr = jnp.right_shift(lax.broadcasted_iota(jnp.int32, (GLA_DV, GLA_DK), 1), 5)
    head_mask = vr == kr

    g_state = gla_st[...]
    s_state = [ssd_st[g] for g in range(SSD_GROUPS)]
    o_parts, y_parts = [], []
    for ci in range(n_chunks):
        r0 = ci * CHUNK
        sl = slice(r0, r0 + CHUNK)
        parts = _split3(decay_in[sl, :])
        sums = _dot(suffix_op, parts[0]) + _dot(suffix_op, parts[1]) + _dot(suffix_op, parts[2])
        rev = sums[0:CHUNK, :]
        tot = sums[CHUNK:CHUNK + SUBLANE, :]

        k_dec = (k[sl, :] * jnp.exp(rev[:, 0:GLA_DK])).astype(BF16)
        upd = jnp.where(head_mask, _dot_tn(v[sl, :], k_dec), 0.0)
        g_dec = jnp.exp(tot[0:1, 0:GLA_DK])
        g_state = g_state * g_dec + upd
        o_parts.append(_dot_nt(q[sl, :], g_state.astype(BF16)))

        w_tok = jnp.exp(rev[:, LANE:2 * LANE]) * dt[sl, :]
        s_dec = jnp.exp(tot[:, LANE:2 * LANE])
        y_g = []
        for g in range(SSD_GROUPS):
            hs = slice(g * LANE, (g + 1) * LANE)
            w_exp = _pair_bcast(w_tok, DT_LANE + 2 * g, CHUNK)
            wx = (xs[sl, hs] * w_exp).astype(BF16)
            upd_s = _dot_tn(bm[sl, hs], wx)
            dec = _pair_bcast(s_dec, DT_LANE + 2 * g, SUBLANE)[0:1, :]
            s_state[g] = s_state[g] * dec + upd_s
            y_g.append(_dot(cm[sl, hs], s_state[g].astype(BF16)))
        y_parts.append(jnp.concatenate(y_g, axis=1))

    gla_st[...] = g_state
    for g in range(SSD_GROUPS):
        ssd_st[g] = s_state[g]

    o = jnp.concatenate(o_parts, axis=0)
    hr = jnp.right_shift(lax.broadcasted_iota(jnp.int32, (GLA_DV, GLA_DV), 0), 6)
    hc = jnp.right_shift(lax.broadcasted_iota(jnp.int32, (GLA_DV, GLA_DV), 1), 6)
    head_mean = jnp.where(hr == hc, 1.0 / (GLA_DV // GLA_HEADS), 0.0).astype(BF16)
    o2 = o * o
    o2_hi = o2.astype(BF16)
    o2_lo = (o2 - o2_hi.astype(F32)).astype(BF16)
    o_ms = _dot(o2_hi, head_mean) + _dot(o2_lo, head_mean)
    y_b = (o * lax.rsqrt(o_ms + EPS) * vec_ref[R_B256:R_B256 + 1, 0:256]
           * _silu(proj[:, C_GZ:C_GZ + BRANCH]))

    y = jnp.concatenate(y_parts, axis=0) + vec_ref[R_B256:R_B256 + 1, 512:768] * xs
    y = y * _silu(proj[:, C_SZ:C_SZ + BRANCH])
    y_ms = jnp.mean(y * y, axis=-1, keepdims=True)
    y_d = y * lax.rsqrt(y_ms + EPS) * vec_ref[R_B256:R_B256 + 1, 768:1024]

    mix = jnp.concatenate([y_a, y_b, y_c, y_d], axis=1).astype(BF16)
    out = x + _dot(mix, w_out_ref[...])
    if final_norm:
        oms = jnp.mean(out * out, axis=-1, keepdims=True)
        out = out * lax.rsqrt(oms + EPS) * vec_ref[R_FINAL:R_FINAL + 1, :]
    o_ref[...] = out


def _layer_call(x2d, w_in, w_out, gate_w, pool_w, vecs, *, batch, final_norm):
    n_tok = x2d.shape[0]
    tiles = n_tok // batch // TILE
    const = lambda b, i: (0, 0)
    return pl.pallas_call(
        functools.partial(_layer_kernel, final_norm=final_norm),
        out_shape=jax.ShapeDtypeStruct(x2d.shape, F32),
        grid=(batch, tiles),
        in_specs=[
            pl.BlockSpec((TILE, D_MODEL), lambda b, i: (b * tiles + i, 0)),
            pl.BlockSpec(w_in.shape, const),
            pl.BlockSpec(w_out.shape, const),
            pl.BlockSpec(gate_w.shape, const),
            pl.BlockSpec(pool_w.shape, const),
            pl.BlockSpec(vecs.shape, const),
        ],
        out_specs=pl.BlockSpec((TILE, D_MODEL), lambda b, i: (b * tiles + i, 0)),
        scratch_shapes=[
            pltpu.VMEM((SUBLANE, BRANCH), F32),
            pltpu.VMEM((SUBLANE, SSD_XBC), F32),
            pltpu.VMEM((2 * SUBLANE, BRANCH), F32),
            pltpu.VMEM((GLA_DV, GLA_DK), F32),
            pltpu.VMEM((SSD_GROUPS, SSD_STATE, LANE), F32),
        ],
        compiler_params=pltpu.CompilerParams(
            dimension_semantics=("arbitrary", "arbitrary"),
            vmem_limit_bytes=VMEM_LIMIT_BYTES),
    )(x2d, w_in, w_out, gate_w, pool_w, vecs)


def _pack_w_in_t(wt):
    glr0, glr1, dt0 = 1536, 1536 + GLA_GATE_RANK, 3344
    pad = jnp.zeros((LANE - GLA_GATE_RANK - SSD_HEADS, wt.shape[1]), wt.dtype)
    return jnp.concatenate([wt[0:glr0], wt[glr1:dt0], wt[glr0:glr1], wt[dt0:], pad],
                           axis=0).astype(BF16)


def _row(vals, width=D_MODEL):
    v = jnp.concatenate([jnp.ravel(p).astype(F32) for p in vals])
    return jnp.pad(v, (0, width - v.shape[0]))[None, :]


def _pack_vecs(layer, norm_w, gla_gate_b, gla_norm_w, pool_scale, conv_a_w, ssd_conv_w, ssd_conv_b,
               ssd_dt_bias, ssd_a_log, ssd_d, ssd_norm_w, final_norm_w):
    lane_pad = jnp.zeros((DT_LANE,), F32)
    tail_pad = jnp.zeros((LANE - DT_LANE - SSD_HEADS,), F32)
    rows = [
        _row([norm_w[layer]]),
        _row([final_norm_w]),
        _row([gla_gate_b[layer], lane_pad, ssd_dt_bias[layer], tail_pad,
              lane_pad, ssd_a_log[layer], tail_pad]),
        _row([jnp.tile(gla_norm_w[layer], GLA_HEADS), pool_scale[layer],
              jnp.repeat(ssd_d[layer], SSD_HEAD_DIM), ssd_norm_w[layer]]),
        _row([ssd_conv_b[layer]]),
    ]
    rows += [_row([conv_a_w[layer, r]]) for r in range(CONV_A_WIDTH)]
    rows += [_row([ssd_conv_w[layer, r]]) for r in range(SSD_CONV)]
    rows.append(jnp.zeros((VEC_ROWS - len(rows), D_MODEL), F32))
    return jnp.concatenate(rows, axis=0)


def _block_diag(blocks):
    n = blocks.shape[0]
    rows = []
    for g in range(n):
        rows.append(jnp.concatenate(
            [blocks[g] if j == g else jnp.zeros_like(blocks[g]) for j in range(n)], axis=1))
    return jnp.concatenate(rows, axis=0)


def kernel(x, norm_w, w_in, conv_a_w, gla_gate_w, gla_gate_b, gla_norm_w, pool_w, pool_scale,
           ssd_conv_w, ssd_conv_b, ssd_dt_bias, ssd_a_log, ssd_d, ssd_norm_w, w_out, final_norm_w):
    batch, seq, d_model = x.shape
    depth = w_in.shape[0]
    assert d_model == D_MODEL and seq % TILE == 0 and TILE % CHUNK == 0
    h = x.reshape(batch * seq, d_model)
    w_in_t = jnp.transpose(w_in, (2, 0, 1))
    for layer in range(depth):
        gate_w = jnp.zeros((LANE, GLA_DK), F32).at[0:GLA_GATE_RANK].set(gla_gate_w[layer]).astype(BF16)
        vecs = _pack_vecs(layer, norm_w, gla_gate_b, gla_norm_w, pool_scale, conv_a_w, ssd_conv_w,
                          ssd_conv_b, ssd_dt_bias, ssd_a_log, ssd_d, ssd_norm_w, final_norm_w)
        h = _layer_call(h, _pack_w_in_t(w_in_t[:, layer, :]), w_out[layer].astype(BF16), gate_w,
                        _block_diag(pool_w[layer]).astype(BF16), vecs,
                        batch=batch, final_norm=(layer == depth - 1))
    return h.reshape(batch, seq, d_model)
```

```python
import functools

import jax
import jax.numpy as jnp
from jax import lax
from jax.experimental import pallas as pl
from jax.experimental.pallas import tpu as pltpu

D_MODEL = 1024
CHUNK = 64
BRANCH = 256
EPS = 1e-6
CONV_A_WIDTH = 3
GLA_HEADS = 4
GLA_DK = 128
GLA_DV = 256
GLA_GATE_RANK = 16
GLA_TAU = 16.0
POOL_WINDOWS = (2, 4, 8, 16)
POOL_GROUP = 64
SSD_HEAD_DIM = 64
SSD_HEADS = 4
SSD_GROUPS = 2
SSD_STATE = 128
SSD_CONV = 4
SSD_XBC = 768

C_AH, C_AB, C_AC, C_AZ = 0, 256, 512, 768
C_Q, C_K, C_V, C_GZ = 1024, 1152, 1280, 1536
C_U, C_PZ = 1792, 2048
C_SZ, C_XBC = 2304, 2560
C_MISC = 3328
D_PROJ_PAD = 3456
DT_LANE = GLA_GATE_RANK

LANE = 128
SUBLANE = 8
TILE = 256
PROJ_BLOCK = 256
N_PROJ_BLOCKS = -(-D_PROJ_PAD // PROJ_BLOCK)
EMIT_PLAN = (2, 2, 3, 3, 2, 2)
VMEM_LIMIT_BYTES = 48 * 1024 * 1024

R_NORM, R_FINAL, R_MISC, R_B256, R_SCB, R_CAW, R_SCW = 0, 1, 2, 3, 4, 5, 8
VEC_ROWS = 16

F32 = jnp.float32
BF16 = jnp.bfloat16


def _silu(x):
    return x * jax.nn.sigmoid(x)


def _log1pexp_neg_abs(x):
    return jnp.log1p(jnp.exp(-jnp.abs(x)))


def _shift_rows(cur, prev, j):
    p = prev.shape[0]
    ext = jnp.concatenate([prev, cur], axis=0)
    return pltpu.roll(ext, j, 0)[p:]


def _split3(x):
    hi = x.astype(BF16)
    r1 = x - hi.astype(F32)
    mid = r1.astype(BF16)
    lo = (r1 - mid.astype(F32)).astype(BF16)
    return hi, mid, lo


def _dot(a, b):
    return jnp.dot(a, b, preferred_element_type=F32)


def _dot_tn(a, b):
    return lax.dot_general(a, b, (((0,), (0,)), ((), ())), preferred_element_type=F32)


def _dot_nt(a, b):
    return lax.dot_general(a, b, (((1,), (1,)), ((), ())), preferred_element_type=F32)


def _pair_bcast(x, lane0, rows):
    lane = lax.broadcasted_iota(jnp.int32, (rows, LANE), 1)
    b0 = jnp.broadcast_to(x[:, lane0:lane0 + 1], (rows, LANE))
    b1 = jnp.broadcast_to(x[:, lane0 + 1:lane0 + 2], (rows, LANE))
    return jnp.where(lane < SSD_HEAD_DIM, b0, b1)


def _start_in_proj(x_ref, w_in_ref, vec_ref, h_ref, proj_ref):
    x = x_ref[...]
    ms = jnp.mean(x * x, axis=-1, keepdims=True)
    h_ref[...] = (x * lax.rsqrt(ms + EPS) * vec_ref[R_NORM:R_NORM + 1, :]).astype(BF16)
    starts = list(range(0, D_PROJ_PAD, PROJ_BLOCK))
    done = [0]

    def emit(n):
        for c0 in starts[done[0]:done[0] + n]:
            c1 = min(c0 + PROJ_BLOCK, D_PROJ_PAD)
            proj_ref[:, c0:c1] = _dot_nt(h_ref[...], w_in_ref[c0:c1, :])
        done[0] = min(done[0] + n, len(starts))

    return emit


def _in_proj(x_ref, w_in_ref, vec_ref, h_ref, proj_ref):
    _start_in_proj(x_ref, w_in_ref, vec_ref, h_ref, proj_ref)(N_PROJ_BLOCKS)


def _constants(t):
    r = lax.broadcasted_iota(jnp.int32, (t, t), 0)
    c = lax.broadcasted_iota(jnp.int32, (t, t), 1)
    same_chunk = jnp.right_shift(r, 6) == jnp.right_shift(c, 6)
    suffix_op = jnp.where(same_chunk & (c > r), 1.0, 0.0).astype(BF16)
    vr = jnp.right_shift(lax.broadcasted_iota(jnp.int32, (GLA_DV, GLA_DK), 0), 6)
    kr = jnp.right_shift(lax.broadcasted_iota(jnp.int32, (GLA_DV, GLA_DK), 1), 5)
    hr = jnp.right_shift(lax.broadcasted_iota(jnp.int32, (GLA_DV, GLA_DV), 0), 6)
    hc = jnp.right_shift(lax.broadcasted_iota(jnp.int32, (GLA_DV, GLA_DV), 1), 6)
    head_mean = jnp.where(hr == hc, 1.0 / (GLA_DV // GLA_HEADS), 0.0).astype(BF16)
    return suffix_op, vr == kr, head_mean


def _mix_and_out(proj_ref, x_ref, out_ref, seq_tile, emit, consts, w_out_ref, gate_w_ref,
                 pool_w_ref, vec_ref, hist_a, hist_x, hist_u, gla_st, ssd_st, *, final_norm):
    t = x_ref.shape[0]
    assert t == TILE
    n_chunks = t // CHUNK
    suffix_op, head_mask, head_mean = consts

    emit(EMIT_PLAN[0])
    misc = proj_ref[:, C_MISC:C_MISC + LANE]
    lane = lax.broadcasted_iota(jnp.int32, (1, LANE), 1)
    dt_mask = (lane >= DT_LANE) & (lane < DT_LANE + SSD_HEADS)
    pre = _dot(misc.astype(BF16), gate_w_ref[...]) + vec_ref[R_MISC:R_MISC + 1, 0:LANE]
    log_a = (jnp.minimum(pre, 0.0) - _log1pexp_neg_abs(pre)) * (1.0 / GLA_TAU)
    dt_pre = misc + vec_ref[R_MISC:R_MISC + 1, LANE:2 * LANE]
    dt = jnp.maximum(dt_pre, 0.0) + _log1pexp_neg_abs(dt_pre)
    a_vec = jnp.where(dt_mask, -jnp.exp(vec_ref[R_MISC:R_MISC + 1, 2 * LANE:3 * LANE]), 0.0)
    d_a = jnp.where(dt_mask, dt * a_vec, 0.0)
    decay_in = jnp.concatenate([log_a, d_a], axis=1)
    parts = _split3(decay_in)

    emit(EMIT_PLAN[1])
    rev = _dot(suffix_op, parts[0]) + _dot(suffix_op, parts[1]) + _dot(suffix_op, parts[2])

    xbc_raw = proj_ref[:, C_XBC:C_XBC + SSD_XBC]
    prev_x = hist_x[...]
    xc = xbc_raw * vec_ref[R_SCW + 3:R_SCW + 4, 0:SSD_XBC] + vec_ref[R_SCB:R_SCB + 1, 0:SSD_XBC]
    for j in (1, 2, 3):
        xc = xc + _shift_rows(xbc_raw, prev_x, j) * vec_ref[R_SCW + 3 - j:R_SCW + 4 - j, 0:SSD_XBC]
    hist_x[...] = xbc_raw[t - SUBLANE:, :]
    xbc = _silu(xc)
    xs = xbc[:, 0:BRANCH]
    bm = xbc[:, BRANCH:BRANCH + SSD_GROUPS * SSD_STATE].astype(BF16)
    cm = xbc[:, BRANCH + SSD_GROUPS * SSD_STATE:].astype(BF16)

    q = (proj_ref[:, C_Q:C_Q + GLA_DK] * ((GLA_DK // GLA_HEADS) ** -0.5)).astype(BF16)
    v = proj_ref[:, C_V:C_V + GLA_DV].astype(BF16)
    k_dec = (proj_ref[:, C_K:C_K + GLA_DK] * jnp.exp(rev[:, 0:GLA_DK])).astype(BF16)
    w_tok = jnp.exp(rev[:, LANE:2 * LANE]) * dt
    wx = jnp.concatenate(
        [xs[:, g * LANE:(g + 1) * LANE] * _pair_bcast(w_tok, DT_LANE + 2 * g, t)
         for g in range(SSD_GROUPS)], axis=1).astype(BF16)

    emit(EMIT_PLAN[2])
    g_upd, s_upd, g_dec, s_dec = [], [], [], []
    for ci in range(n_chunks):
        r0 = ci * CHUNK
        sl = slice(r0, r0 + CHUNK)
        g_upd.append(_dot_tn(v[sl, :], k_dec[sl, :]))
        s_upd.append([_dot_tn(bm[sl, g * LANE:(g + 1) * LANE], wx[sl, g * LANE:(g + 1) * LANE])
                      for g in range(SSD_GROUPS)])
        tot = rev[r0:r0 + 1, :] + decay_in[r0:r0 + 1, :]
        g_dec.append(jnp.exp(tot[:, 0:GLA_DK]))
        dec = jnp.exp(tot[:, LANE:2 * LANE])
        s_dec.append([_pair_bcast(dec, DT_LANE + 2 * g, 1) for g in range(SSD_GROUPS)])

    c = proj_ref[:, C_AC:C_AC + BRANCH] * proj_ref[:, C_AH:C_AH + BRANCH]
    prev_a = hist_a[...]
    conv = c * vec_ref[R_CAW + 2:R_CAW + 3, 0:BRANCH]
    conv = conv + _shift_rows(c, prev_a, 1) * vec_ref[R_CAW + 1:R_CAW + 2, 0:BRANCH]
    conv = conv + _shift_rows(c, prev_a, 2) * vec_ref[R_CAW:R_CAW + 1, 0:BRANCH]
    hist_a[...] = c[t - SUBLANE:, :]
    y_a = proj_ref[:, C_AB:C_AB + BRANCH] * conv * _silu(proj_ref[:, C_AZ:C_AZ + BRANCH])

    u = proj_ref[:, C_U:C_U + BRANCH]
    hp = hist_u.shape[0]
    lane_u = lax.broadcasted_iota(jnp.int32, (1, BRANCH), 1)
    grp = jnp.right_shift(lane_u, 6)
    s = jnp.concatenate([hist_u[...], u], axis=0)
    s = s + pltpu.roll(s, 1, 0)
    s = s + jnp.where(grp >= 1, pltpu.roll(s, 2, 0), 0.0)
    s = s + jnp.where(grp >= 2, pltpu.roll(s, 4, 0), 0.0)
    s = s + jnp.where(grp >= 3, pltpu.roll(s, 8, 0), 0.0)
    win = s[hp:, :]
    hist_u[...] = u[t - hp:, :]
    window = jnp.left_shift(2, grp)
    pos = seq_tile * t + lax.broadcasted_iota(jnp.int32, (t, BRANCH), 0)
    cnt = jnp.minimum(pos + 1, window).astype(F32)
    pooled = win / cnt - u
    mixed = _dot(pooled.astype(BF16), pool_w_ref[...])
    y_c = vec_ref[R_B256:R_B256 + 1, 256:512] * mixed * _silu(proj_ref[:, C_PZ:C_PZ + BRANCH])

    emit(EMIT_PLAN[3])
    g_state = gla_st[...]
    s_state = [ssd_st[g] for g in range(SSD_GROUPS)]
    o_parts, y_parts = [], []
    for ci in range(n_chunks):
        sl = slice(ci * CHUNK, (ci + 1) * CHUNK)
        g_state = g_state * g_dec[ci] + jnp.where(head_mask, g_upd[ci], 0.0)
        o_parts.append(_dot_nt(q[sl, :], g_state.astype(BF16)))
        y_g = []
        for g in range(SSD_GROUPS):
            s_state[g] = s_state[g] * s_dec[ci][g] + s_upd[ci][g]
            y_g.append(_dot(cm[sl, g * LANE:(g + 1) * LANE], s_state[g].astype(BF16)))
        y_parts.append(jnp.concatenate(y_g, axis=1))
    gla_st[...] = g_state
    for g in range(SSD_GROUPS):
        ssd_st[g] = s_state[g]

    emit(EMIT_PLAN[4])
    o = jnp.concatenate(o_parts, axis=0)
    o2 = o * o
    o2_hi = o2.astype(BF16)
    o2_lo = (o2 - o2_hi.astype(F32)).astype(BF16)
    o_ms = _dot(o2_hi, head_mean) + _dot(o2_lo, head_mean)
    y_b = (o * lax.rsqrt(o_ms + EPS) * vec_ref[R_B256:R_B256 + 1, 0:256]
           * _silu(proj_ref[:, C_GZ:C_GZ + BRANCH]))

    y = jnp.concatenate(y_parts, axis=0) + vec_ref[R_B256:R_B256 + 1, 512:768] * xs
    y = y * _silu(proj_ref[:, C_SZ:C_SZ + BRANCH])
    y_ms = jnp.mean(y * y, axis=-1, keepdims=True)
    y_d = y * lax.rsqrt(y_ms + EPS) * vec_ref[R_B256:R_B256 + 1, 768:1024]

    emit(EMIT_PLAN[5])
    mix = jnp.concatenate([y_a, y_b, y_c, y_d], axis=1).astype(BF16)
    out = x_ref[...] + _dot(mix, w_out_ref[...])
    if final_norm:
        oms = jnp.mean(out * out, axis=-1, keepdims=True)
        out = out * lax.rsqrt(oms + EPS) * vec_ref[R_FINAL:R_FINAL + 1, :]
    out_ref[...] = out
    emit(N_PROJ_BLOCKS)


def _layer_kernel(x0_ref, x1_ref, x2_ref, w_in_ref, w_out_ref, gate_w_ref, pool_w_ref, vec_ref,
                  o_ref, proj0, proj1, h_buf, hist_a, hist_x, hist_u, gla_st, ssd_st,
                  *, steps_per_seq, final_norm):
    t = x0_ref.shape[0]
    j = pl.program_id(0)
    seq_step = lax.rem(j, steps_per_seq)

    @pl.when(seq_step == 0)
    def _():
        hist_a[...] = jnp.zeros_like(hist_a)
        hist_x[...] = jnp.zeros_like(hist_x)
        hist_u[...] = jnp.zeros_like(hist_u)
        gla_st[...] = jnp.zeros_like(gla_st)
        ssd_st[...] = jnp.zeros_like(ssd_st)

    @pl.when(j == 0)
    def _():
        _in_proj(x0_ref, w_in_ref, vec_ref, h_buf, proj0)

    mix = functools.partial(
        _mix_and_out, consts=_constants(t), w_out_ref=w_out_ref, gate_w_ref=gate_w_ref,
        pool_w_ref=pool_w_ref, vec_ref=vec_ref, hist_a=hist_a, hist_x=hist_x, hist_u=hist_u,
        gla_st=gla_st, ssd_st=ssd_st, final_norm=final_norm)
    emit = _start_in_proj(x1_ref, w_in_ref, vec_ref, h_buf, proj1)
    mix(proj0, x0_ref, o_ref.at[pl.ds(0, t)], 2 * seq_step, emit)
    emit = _start_in_proj(x2_ref, w_in_ref, vec_ref, h_buf, proj0)
    mix(proj1, x1_ref, o_ref.at[pl.ds(t, t)], 2 * seq_step + 1, emit)


def _layer_call(x2d, w_in, w_out, gate_w, pool_w, vecs, *, batch, final_norm):
    n_tiles = x2d.shape[0] // TILE
    tiles_per_seq = n_tiles // batch
    assert tiles_per_seq % 2 == 0
    steps = n_tiles // 2
    const = lambda j: (0, 0)
    x_spec = lambda f: pl.BlockSpec((TILE, D_MODEL), f)
    return pl.pallas_call(
        functools.partial(_layer_kernel, steps_per_seq=tiles_per_seq // 2, final_norm=final_norm),
        out_shape=jax.ShapeDtypeStruct(x2d.shape, F32),
        grid=(steps,),
        in_specs=[
            x_spec(lambda j: (2 * j, 0)),
            x_spec(lambda j: (2 * j + 1, 0)),
            x_spec(lambda j: (jnp.minimum(2 * j + 2, n_tiles - 1), 0)),
            pl.BlockSpec(w_in.shape, const),
            pl.BlockSpec(w_out.shape, const),
            pl.BlockSpec(gate_w.shape, const),
            pl.BlockSpec(pool_w.shape, const),
            pl.BlockSpec(vecs.shape, const),
        ],
        out_specs=pl.BlockSpec((2 * TILE, D_MODEL), lambda j: (j, 0)),
        scratch_shapes=[
            pltpu.VMEM((TILE, D_PROJ_PAD), F32),
            pltpu.VMEM((TILE, D_PROJ_PAD), F32),
            pltpu.VMEM((TILE, D_MODEL), BF16),
            pltpu.VMEM((SUBLANE, BRANCH), F32),
            pltpu.VMEM((SUBLANE, SSD_XBC), F32),
            pltpu.VMEM((2 * SUBLANE, BRANCH), F32),
            pltpu.VMEM((GLA_DV, GLA_DK), F32),
            pltpu.VMEM((SSD_GROUPS, SSD_STATE, LANE), F32),
        ],
        compiler_params=pltpu.CompilerParams(
            dimension_semantics=("arbitrary",),
            vmem_limit_bytes=VMEM_LIMIT_BYTES),
    )(x2d, x2d, x2d, w_in, w_out, gate_w, pool_w, vecs)


def _pack_w_in_t(wt):
    glr0, glr1, dt0 = 1536, 1536 + GLA_GATE_RANK, 3344
    pad = jnp.zeros((LANE - GLA_GATE_RANK - SSD_HEADS, wt.shape[1]), wt.dtype)
    return jnp.concatenate([wt[0:glr0], wt[glr1:dt0], wt[glr0:glr1], wt[dt0:], pad],
                           axis=0).astype(BF16)


def _row(vals, width=D_MODEL):
    v = jnp.concatenate([jnp.ravel(p).astype(F32) for p in vals])
    return jnp.pad(v, (0, width - v.shape[0]))[None, :]


def _pack_vecs(layer, norm_w, gla_gate_b, gla_norm_w, pool_scale, conv_a_w, ssd_conv_w, ssd_conv_b,
               ssd_dt_bias, ssd_a_log, ssd_d, ssd_norm_w, final_norm_w):
    lane_pad = jnp.zeros((DT_LANE,), F32)
    tail_pad = jnp.zeros((LANE - DT_LANE - SSD_HEADS,), F32)
    rows = [
        _row([norm_w[layer]]),
        _row([final_norm_w]),
        _row([gla_gate_b[layer], lane_pad, ssd_dt_bias[layer], tail_pad,
              lane_pad, ssd_a_log[layer], tail_pad]),
        _row([jnp.tile(gla_norm_w[layer], GLA_HEADS), pool_scale[layer],
              jnp.repeat(ssd_d[layer], SSD_HEAD_DIM), ssd_norm_w[layer]]),
        _row([ssd_conv_b[layer]]),
    ]
    rows += [_row([conv_a_w[layer, r]]) for r in range(CONV_A_WIDTH)]
    rows += [_row([ssd_conv_w[layer, r]]) for r in range(SSD_CONV)]
    rows.append(jnp.zeros((VEC_ROWS - len(rows), D_MODEL), F32))
    return jnp.concatenate(rows, axis=0)


def _block_diag(blocks):
    n = blocks.shape[0]
    rows = []
    for g in range(n):
        rows.append(jnp.concatenate(
            [blocks[g] if j == g else jnp.zeros_like(blocks[g]) for j in range(n)], axis=1))
    return jnp.concatenate(rows, axis=0)


def kernel(x, norm_w, w_in, conv_a_w, gla_gate_w, gla_gate_b, gla_norm_w, pool_w, pool_scale,
           ssd_conv_w, ssd_conv_b, ssd_dt_bias, ssd_a_log, ssd_d, ssd_norm_w, w_out, final_norm_w):
    batch, seq, d_model = x.shape
    depth = w_in.shape[0]
    assert d_model == D_MODEL and seq % (2 * TILE) == 0 and TILE % CHUNK == 0
    h = x.reshape(batch * seq, d_model)
    w_in_t = jnp.transpose(w_in, (2, 0, 1))
    for layer in range(depth):
        gate_w = jnp.zeros((LANE, GLA_DK), F32).at[0:GLA_GATE_RANK].set(gla_gate_w[layer]).astype(BF16)
        vecs = _pack_vecs(layer, norm_w, gla_gate_b, gla_norm_w, pool_scale, conv_a_w, ssd_conv_w,
                          ssd_conv_b, ssd_dt_bias, ssd_a_log, ssd_d, ssd_norm_w, final_norm_w)
        h = _layer_call(h, _pack_w_in_t(w_in_t[:, layer, :]), w_out[layer].astype(BF16), gate_w,
                        _block_diag(pool_w[layer]).astype(BF16), vecs,
                        batch=batch, final_norm=(layer == depth - 1))
    return h.reshape(batch, seq, d_model)
```

```python
import functools

import jax
import jax.numpy as jnp
from jax import lax
from jax.experimental import pallas as pl
from jax.experimental.pallas import tpu as pltpu

D_MODEL = 1024
CHUNK = 64
BRANCH = 256
EPS = 1e-6
CONV_A_WIDTH = 3
GLA_HEADS = 4
GLA_DK = 128
GLA_DV = 256
GLA_GATE_RANK = 16
GLA_TAU = 16.0
POOL_WINDOWS = (2, 4, 8, 16)
POOL_GROUP = 64
SSD_HEAD_DIM = 64
SSD_HEADS = 4
SSD_GROUPS = 2
SSD_STATE = 128
SSD_CONV = 4
SSD_XBC = 768

C_AH, C_AB, C_AC, C_AZ = 0, 256, 512, 768
C_Q, C_K, C_V, C_GZ = 1024, 1152, 1280, 1536
C_U, C_PZ = 1792, 2048
C_SZ, C_XBC = 2304, 2560
C_MISC = 3328
D_PROJ_PAD = 3456
DT_LANE = GLA_GATE_RANK

LANE = 128
SUBLANE = 8
TILE = 256
PROJ_BLOCK = 512
N_PROJ_BLOCKS = -(-D_PROJ_PAD // PROJ_BLOCK)
EMIT_PLAN = (1, 1, 2, 1, 1, 1)
VMEM_LIMIT_BYTES = 48 * 1024 * 1024

R_NORM, R_FINAL, R_MISC, R_B256, R_SCB, R_CAW, R_SCW = 0, 1, 2, 3, 4, 5, 8
VEC_ROWS = 16

F32 = jnp.float32
BF16 = jnp.bfloat16


def _silu(x):
    return x * jax.nn.sigmoid(x)


def _log1pexp_neg_abs(x):
    return jnp.log1p(jnp.exp(-jnp.abs(x)))


def _shift_rows(cur, prev, j):
    p = prev.shape[0]
    ext = jnp.concatenate([prev, cur], axis=0)
    return pltpu.roll(ext, j, 0)[p:]


def _split3(x):
    hi = x.astype(BF16)
    r1 = x - hi.astype(F32)
    mid = r1.astype(BF16)
    lo = (r1 - mid.astype(F32)).astype(BF16)
    return hi, mid, lo


def _dot(a, b):
    return jnp.dot(a, b, preferred_element_type=F32)


def _dot_tn(a, b):
    return lax.dot_general(a, b, (((0,), (0,)), ((), ())), preferred_element_type=F32)


def _dot_nt(a, b):
    return lax.dot_general(a, b, (((1,), (1,)), ((), ())), preferred_element_type=F32)


def _pair_bcast(x, lane0, rows):
    lane = lax.broadcasted_iota(jnp.int32, (rows, LANE), 1)
    b0 = jnp.broadcast_to(x[:, lane0:lane0 + 1], (rows, LANE))
    b1 = jnp.broadcast_to(x[:, lane0 + 1:lane0 + 2], (rows, LANE))
    return jnp.where(lane < SSD_HEAD_DIM, b0, b1)


def _start_in_proj(x_ref, w_in_ref, vec_ref, h_ref, proj_ref):
    x = x_ref[...]
    ms = jnp.mean(x * x, axis=-1, keepdims=True)
    h_ref[...] = (x * lax.rsqrt(ms + EPS) * vec_ref[R_NORM:R_NORM + 1, :]).astype(BF16)
    starts = list(range(0, D_PROJ_PAD, PROJ_BLOCK))
    done = [0]

    def emit(n):
        for c0 in starts[done[0]:done[0] + n]:
            c1 = min(c0 + PROJ_BLOCK, D_PROJ_PAD)
            proj_ref[:, c0:c1] = _dot(h_ref[...], w_in_ref[:, c0:c1])
        done[0] = min(done[0] + n, len(starts))

    return emit


def _in_proj(x_ref, w_in_ref, vec_ref, h_ref, proj_ref):
    _start_in_proj(x_ref, w_in_ref, vec_ref, h_ref, proj_ref)(N_PROJ_BLOCKS)


def _constants(t):
    r = lax.broadcasted_iota(jnp.int32, (t, t), 0)
    c = lax.broadcasted_iota(jnp.int32, (t, t), 1)
    same_chunk = jnp.right_shift(r, 6) == jnp.right_shift(c, 6)
    suffix_op = jnp.where(same_chunk & (c > r), 1.0, 0.0).astype(BF16)
    vr = jnp.right_shift(lax.broadcasted_iota(jnp.int32, (GLA_DV, GLA_DK), 0), 6)
    kr = jnp.right_shift(lax.broadcasted_iota(jnp.int32, (GLA_DV, GLA_DK), 1), 5)
    return suffix_op, vr == kr


def _mix_and_out(proj_ref, x_ref, out_ref, seq_tile, emit, consts, w_out_ref, gate_w_ref,
                 pool_w_ref, vec_ref, hist_a, hist_x, hist_u, gla_st, ssd_st, *, final_norm):
    t = x_ref.shape[0]
    assert t == TILE
    n_chunks = t // CHUNK
    suffix_op, head_mask = consts

    emit(EMIT_PLAN[0])
    misc = proj_ref[:, C_MISC:C_MISC + LANE]
    lane = lax.broadcasted_iota(jnp.int32, (1, LANE), 1)
    dt_mask = (lane >= DT_LANE) & (lane < DT_LANE + SSD_HEADS)
    pre = _dot(misc.astype(BF16), gate_w_ref[...]) + vec_ref[R_MISC:R_MISC + 1, 0:LANE]
    log_a = (jnp.minimum(pre, 0.0) - _log1pexp_neg_abs(pre)) * (1.0 / GLA_TAU)
    dt_pre = misc + vec_ref[R_MISC:R_MISC + 1, LANE:2 * LANE]
    dt = jnp.maximum(dt_pre, 0.0) + _log1pexp_neg_abs(dt_pre)
    a_vec = jnp.where(dt_mask, -jnp.exp(vec_ref[R_MISC:R_MISC + 1, 2 * LANE:3 * LANE]), 0.0)
    d_a = jnp.where(dt_mask, dt * a_vec, 0.0)
    decay_in = jnp.concatenate([log_a, d_a], axis=1)
    parts = _split3(decay_in)

    emit(EMIT_PLAN[1])
    rev = _dot(suffix_op, parts[0]) + _dot(suffix_op, parts[1]) + _dot(suffix_op, parts[2])

    xbc_raw = proj_ref[:, C_XBC:C_XBC + SSD_XBC]
    prev_x = hist_x[...]
    xc = xbc_raw * vec_ref[R_SCW + 3:R_SCW + 4, 0:SSD_XBC] + vec_ref[R_SCB:R_SCB + 1, 0:SSD_XBC]
    for j in (1, 2, 3):
        xc = xc + _shift_rows(xbc_raw, prev_x, j) * vec_ref[R_SCW + 3 - j:R_SCW + 4 - j, 0:SSD_XBC]
    hist_x[...] = xbc_raw[t - SUBLANE:, :]
    xbc = _silu(xc)
    xs = xbc[:, 0:BRANCH]
    bm = xbc[:, BRANCH:BRANCH + SSD_GROUPS * SSD_STATE].astype(BF16)
    cm = xbc[:, BRANCH + SSD_GROUPS * SSD_STATE:].astype(BF16)

    q = (proj_ref[:, C_Q:C_Q + GLA_DK] * ((GLA_DK // GLA_HEADS) ** -0.5)).astype(BF16)
    v = proj_ref[:, C_V:C_V + GLA_DV].astype(BF16)
    k_dec = (proj_ref[:, C_K:C_K + GLA_DK] * jnp.exp(rev[:, 0:GLA_DK])).astype(BF16)
    w_tok = jnp.exp(rev[:, LANE:2 * LANE]) * dt
    wx = jnp.concatenate(
        [xs[:, g * LANE:(g + 1) * LANE] * _pair_bcast(w_tok, DT_LANE + 2 * g, t)
         for g in range(SSD_GROUPS)], axis=1).astype(BF16)

    emit(EMIT_PLAN[2])
    def pair_rhs(a, r0):
        z = jnp.zeros((CHUNK, LANE), BF16)
        return jnp.concatenate([jnp.concatenate([a[r0:r0 + CHUNK], z], axis=1),
                                jnp.concatenate([z, a[r0 + CHUNK:r0 + 2 * CHUNK]], axis=1)], axis=0)

    g_upd, s_upd, g_dec, s_dec = [], [], [], []
    for r0 in range(0, t, 2 * CHUNK):
        pr = slice(r0, r0 + 2 * CHUNK)
        gu = _dot_tn(v[pr, :], pair_rhs(k_dec, r0))
        su = [_dot_tn(bm[pr, g * LANE:(g + 1) * LANE], pair_rhs(wx[:, g * LANE:(g + 1) * LANE], r0))
              for g in range(SSD_GROUPS)]
        for half in range(2):
            hs = slice(half * LANE, (half + 1) * LANE)
            g_upd.append(gu[:, hs])
            s_upd.append([su[g][:, hs] for g in range(SSD_GROUPS)])
    for ci in range(n_chunks):
        r0 = ci * CHUNK
        tot = rev[r0:r0 + 1, :] + decay_in[r0:r0 + 1, :]
        g_dec.append(jnp.exp(tot[:, 0:GLA_DK]))
        dec = jnp.exp(tot[:, LANE:2 * LANE])
        s_dec.append([_pair_bcast(dec, DT_LANE + 2 * g, 1) for g in range(SSD_GROUPS)])

    c = proj_ref[:, C_AC:C_AC + BRANCH] * proj_ref[:, C_AH:C_AH + BRANCH]
    prev_a = hist_a[...]
    conv = c * vec_ref[R_CAW + 2:R_CAW + 3, 0:BRANCH]
    conv = conv + _shift_rows(c, prev_a, 1) * vec_ref[R_CAW + 1:R_CAW + 2, 0:BRANCH]
    conv = conv + _shift_rows(c, prev_a, 2) * vec_ref[R_CAW:R_CAW + 1, 0:BRANCH]
    hist_a[...] = c[t - SUBLANE:, :]
    y_a = proj_ref[:, C_AB:C_AB + BRANCH] * conv * _silu(proj_ref[:, C_AZ:C_AZ + BRANCH])

    u = proj_ref[:, C_U:C_U + BRANCH]
    hp = hist_u.shape[0]
    lane_u = lax.broadcasted_iota(jnp.int32, (1, BRANCH), 1)
    grp = jnp.right_shift(lane_u, 6)
    s = jnp.concatenate([hist_u[...], u], axis=0)
    s = s + pltpu.roll(s, 1, 0)
    s = s + jnp.where(grp >= 1, pltpu.roll(s, 2, 0), 0.0)
    s = s + jnp.where(grp >= 2, pltpu.roll(s, 4, 0), 0.0)
    s = s + jnp.where(grp >= 3, pltpu.roll(s, 8, 0), 0.0)
    win = s[hp:, :]
    hist_u[...] = u[t - hp:, :]
    window = jnp.left_shift(2, grp)
    pos = seq_tile * t + lax.broadcasted_iota(jnp.int32, (t, BRANCH), 0)
    cnt = jnp.minimum(pos + 1, window).astype(F32)
    pooled = win / cnt - u
    mixed = _dot(pooled.astype(BF16), pool_w_ref[...])
    y_c = vec_ref[R_B256:R_B256 + 1, 256:512] * mixed * _silu(proj_ref[:, C_PZ:C_PZ + BRANCH])

    emit(EMIT_PLAN[3])
    g_state = gla_st[...]
    s_state = [ssd_st[g] for g in range(SSD_GROUPS)]
    o_parts, y_parts = [], []
    zero_blk = jnp.zeros((SSD_STATE, LANE), BF16)
    for ci in range(n_chunks):
        sl = slice(ci * CHUNK, (ci + 1) * CHUNK)
        g_state = g_state * g_dec[ci] + jnp.where(head_mask, g_upd[ci], 0.0)
        o_parts.append(_dot_nt(q[sl, :], g_state.astype(BF16)))
        for g in range(SSD_GROUPS):
            s_state[g] = s_state[g] * s_dec[ci][g] + s_upd[ci][g]
        s_bd = jnp.concatenate(
            [jnp.concatenate([s_state[0].astype(BF16), zero_blk], axis=1),
             jnp.concatenate([zero_blk, s_state[1].astype(BF16)], axis=1)], axis=0)
        y_parts.append(_dot(cm[sl, :], s_bd))
    gla_st[...] = g_state
    for g in range(SSD_GROUPS):
        ssd_st[g] = s_state[g]

    emit(EMIT_PLAN[4])
    o = jnp.concatenate(o_parts, axis=0)
    o2 = o * o
    lane_lo = lax.broadcasted_iota(jnp.int32, (1, LANE), 1) < (GLA_DV // GLA_HEADS)
    ms_tiles = []
    for j in range(GLA_DV // LANE):
        o2_j = o2[:, j * LANE:(j + 1) * LANE]
        s_lo = jnp.sum(jnp.where(lane_lo, o2_j, 0.0), axis=-1, keepdims=True)
        s_hi = jnp.sum(jnp.where(lane_lo, 0.0, o2_j), axis=-1, keepdims=True)
        ms_tiles.append(jnp.where(lane_lo, s_lo, s_hi))
    o_ms = jnp.concatenate(ms_tiles, axis=1) * (1.0 / (GLA_DV // GLA_HEADS))
    y_b = (o * lax.rsqrt(o_ms + EPS) * vec_ref[R_B256:R_B256 + 1, 0:256]
           * _silu(proj_ref[:, C_GZ:C_GZ + BRANCH]))

    y = jnp.concatenate(y_parts, axis=0) + vec_ref[R_B256:R_B256 + 1, 512:768] * xs
    y = y * _silu(proj_ref[:, C_SZ:C_SZ + BRANCH])
    y_ms = jnp.mean(y * y, axis=-1, keepdims=True)
    y_d = y * lax.rsqrt(y_ms + EPS) * vec_ref[R_B256:R_B256 + 1, 768:1024]

    emit(EMIT_PLAN[5])
    mix = jnp.concatenate([y_a, y_b, y_c, y_d], axis=1).astype(BF16)
    out = x_ref[...] + _dot(mix, w_out_ref[...])
    if final_norm:
        oms = jnp.mean(out * out, axis=-1, keepdims=True)
        out = out * lax.rsqrt(oms + EPS) * vec_ref[R_FINAL:R_FINAL + 1, :]
    out_ref[...] = out
    emit(N_PROJ_BLOCKS)


def _layer_kernel(x0_ref, x1_ref, x2_ref, w_in_ref, w_out_ref, gate_w_ref, pool_w_ref, vec_ref,
                  o_ref, proj0, proj1, h_buf, hist_a, hist_x, hist_u, gla_st, ssd_st,
                  *, steps_per_seq, final_norm):
    t = x0_ref.shape[0]
    j = pl.program_id(0)
    seq_step = lax.rem(j, steps_per_seq)

    @pl.when(seq_step == 0)
    def _():
        hist_a[...] = jnp.zeros_like(hist_a)
        hist_x[...] = jnp.zeros_like(hist_x)
        hist_u[...] = jnp.zeros_like(hist_u)
        gla_st[...] = jnp.zeros_like(gla_st)
        ssd_st[...] = jnp.zeros_like(ssd_st)

    @pl.when(j == 0)
    def _():
        _in_proj(x0_ref, w_in_ref, vec_ref, h_buf, proj0)

    mix = functools.partial(
        _mix_and_out, consts=_constants(t), w_out_ref=w_out_ref, gate_w_ref=gate_w_ref,
        pool_w_ref=pool_w_ref, vec_ref=vec_ref, hist_a=hist_a, hist_x=hist_x, hist_u=hist_u,
        gla_st=gla_st, ssd_st=ssd_st, final_norm=final_norm)
    emit = _start_in_proj(x1_ref, w_in_ref, vec_ref, h_buf, proj1)
    mix(proj0, x0_ref, o_ref.at[pl.ds(0, t)], 2 * seq_step, emit)
    emit = _start_in_proj(x2_ref, w_in_ref, vec_ref, h_buf, proj0)
    mix(proj1, x1_ref, o_ref.at[pl.ds(t, t)], 2 * seq_step + 1, emit)


def _layer_call(x2d, w_in, w_out, gate_w, pool_w, vecs, *, batch, final_norm):
    n_tiles = x2d.shape[0] // TILE
    tiles_per_seq = n_tiles // batch
    assert tiles_per_seq % 2 == 0
    steps = n_tiles // 2
    const = lambda j: (0, 0)
    x_spec = lambda f: pl.BlockSpec((TILE, D_MODEL), f)
    return pl.pallas_call(
        functools.partial(_layer_kernel, steps_per_seq=tiles_per_seq // 2, final_norm=final_norm),
        out_shape=jax.ShapeDtypeStruct(x2d.shape, F32),
        grid=(steps,),
        in_specs=[
            x_spec(lambda j: (2 * j, 0)),
            x_spec(lambda j: (2 * j + 1, 0)),
            x_spec(lambda j: (jnp.minimum(2 * j + 2, n_tiles - 1), 0)),
            pl.BlockSpec(w_in.shape, const),
            pl.BlockSpec(w_out.shape, const),
            pl.BlockSpec(gate_w.shape, const),
            pl.BlockSpec(pool_w.shape, const),
            pl.BlockSpec(vecs.shape, const),
        ],
        out_specs=pl.BlockSpec((2 * TILE, D_MODEL), lambda j: (j, 0)),
        scratch_shapes=[
            pltpu.VMEM((TILE, D_PROJ_PAD), F32),
            pltpu.VMEM((TILE, D_PROJ_PAD), F32),
            pltpu.VMEM((TILE, D_MODEL), BF16),
            pltpu.VMEM((SUBLANE, BRANCH), F32),
            pltpu.VMEM((SUBLANE, SSD_XBC), F32),
            pltpu.VMEM((2 * SUBLANE, BRANCH), F32),
            pltpu.VMEM((GLA_DV, GLA_DK), F32),
            pltpu.VMEM((SSD_GROUPS, SSD_STATE, LANE), F32),
        ],
        compiler_params=pltpu.CompilerParams(
            dimension_semantics=("arbitrary",),
            vmem_limit_bytes=VMEM_LIMIT_BYTES),
    )(x2d, x2d, x2d, w_in, w_out, gate_w, pool_w, vecs)


def _w_in_prep_kernel(wt_ref, misc_ref, o_ref):
    for c0 in range(0, C_MISC, LANE):
        src = c0 if c0 < C_GZ else c0 + GLA_GATE_RANK
        o_ref[:, c0:c0 + LANE] = wt_ref[src:src + LANE, :].T
    o_ref[:, C_MISC:C_MISC + LANE] = misc_ref[...].T


def _prep_w_in(wt, misc):
    return pl.pallas_call(
        _w_in_prep_kernel,
        out_shape=jax.ShapeDtypeStruct((D_MODEL, D_PROJ_PAD), BF16),
        compiler_params=pltpu.CompilerParams(vmem_limit_bytes=VMEM_LIMIT_BYTES),
    )(wt, misc)


def _misc_rows(wt):
    glr0, dt0 = C_GZ, 3344
    pad = jnp.zeros((LANE - GLA_GATE_RANK - SSD_HEADS, wt.shape[1]), wt.dtype)
    return jnp.concatenate([wt[glr0:glr0 + GLA_GATE_RANK], wt[dt0:dt0 + SSD_HEADS], pad], axis=0)


def _row(vals, width=D_MODEL):
    v = jnp.concatenate([jnp.ravel(p).astype(F32) for p in vals])
    return jnp.pad(v, (0, width - v.shape[0]))[None, :]


def _pack_vecs(layer, norm_w, gla_gate_b, gla_norm_w, pool_scale, conv_a_w, ssd_conv_w, ssd_conv_b,
               ssd_dt_bias, ssd_a_log, ssd_d, ssd_norm_w, final_norm_w):
    lane_pad = jnp.zeros((DT_LANE,), F32)
    tail_pad = jnp.zeros((LANE - DT_LANE - SSD_HEADS,), F32)
    rows = [
        _row([norm_w[layer]]),
        _row([final_norm_w]),
        _row([gla_gate_b[layer], lane_pad, ssd_dt_bias[layer], tail_pad,
              lane_pad, ssd_a_log[layer], tail_pad]),
        _row([jnp.tile(gla_norm_w[layer], GLA_HEADS), pool_scale[layer],
              jnp.repeat(ssd_d[layer], SSD_HEAD_DIM), ssd_norm_w[layer]]),
        _row([ssd_conv_b[layer]]),
    ]
    rows += [_row([conv_a_w[layer, r]]) for r in range(CONV_A_WIDTH)]
    rows += [_row([ssd_conv_w[layer, r]]) for r in range(SSD_CONV)]
    rows.append(jnp.zeros((VEC_ROWS - len(rows), D_MODEL), F32))
    return jnp.concatenate(rows, axis=0)


def _block_diag(blocks):
    n = blocks.shape[0]
    rows = []
    for g in range(n):
        rows.append(jnp.concatenate(
            [blocks[g] if j == g else jnp.zeros_like(blocks[g]) for j in range(n)], axis=1))
    return jnp.concatenate(rows, axis=0)


def kernel(x, norm_w, w_in, conv_a_w, gla_gate_w, gla_gate_b, gla_norm_w, pool_w, pool_scale,
           ssd_conv_w, ssd_conv_b, ssd_dt_bias, ssd_a_log, ssd_d, ssd_norm_w, w_out, final_norm_w):
    batch, seq, d_model = x.shape
    depth = w_in.shape[0]
    assert d_model == D_MODEL and seq % (2 * TILE) == 0 and TILE % CHUNK == 0
    h = x.reshape(batch * seq, d_model)
    w_in_t = jnp.transpose(w_in, (2, 0, 1))
    for layer in range(depth):
        gate_w = jnp.zeros((LANE, GLA_DK), F32).at[0:GLA_GATE_RANK].set(gla_gate_w[layer]).astype(BF16)
        vecs = _pack_vecs(layer, norm_w, gla_gate_b, gla_norm_w, pool_scale, conv_a_w, ssd_conv_w,
                          ssd_conv_b, ssd_dt_bias, ssd_a_log, ssd_d, ssd_norm_w, final_norm_w)
        wt = w_in_t[:, layer, :].astype(BF16)
        h = _layer_call(h, _prep_w_in(wt, _misc_rows(wt)), w_out[layer].astype(BF16), gate_w,
                        _block_diag(pool_w[layer]).astype(BF16), vecs,
                        batch=batch, final_norm=(layer == depth - 1))
    return h.reshape(batch, seq, d_model)
```

```python
import functools

import jax
import jax.numpy as jnp
from jax import lax
from jax.experimental import pallas as pl
from jax.experimental.pallas import tpu as pltpu

D_MODEL = 1024
CHUNK = 64
BRANCH = 256
EPS = 1e-6
CONV_A_WIDTH = 3
GLA_HEADS = 4
GLA_DK = 128
GLA_DV = 256
GLA_GATE_RANK = 16
GLA_TAU = 16.0
POOL_WINDOWS = (2, 4, 8, 16)
POOL_GROUP = 64
SSD_HEAD_DIM = 64
SSD_HEADS = 4
SSD_GROUPS = 2
SSD_STATE = 128
SSD_CONV = 4
SSD_XBC = 768

C_AH, C_AB, C_AC, C_AZ = 0, 256, 512, 768
C_Q, C_K, C_V, C_GZ = 1024, 1152, 1280, 1536
C_U, C_PZ = 1792, 2048
C_SZ, C_XBC = 2304, 2560
C_GATE = 3328
C_MISC = 3456
D_PROJ_PAD = 3584
DT_LANE = 16
D_PROJ_SRC = 3348

LANE = 128
SUBLANE = 8
TILE = 256
PROJ_BLOCK = 512
N_PROJ_BLOCKS = -(-D_PROJ_PAD // PROJ_BLOCK)
EMIT_PLAN = (1, 1, 2, 1, 1, 1)
VMEM_LIMIT_BYTES = 48 * 1024 * 1024

R_NORM, R_FINAL, R_MISC, R_B256, R_SCB, R_CAW, R_SCW = 0, 1, 2, 3, 4, 5, 8
VEC_ROWS = 16

F32 = jnp.float32
BF16 = jnp.bfloat16


def _silu(x):
    return x * jax.nn.sigmoid(x)


def _log1pexp_neg_abs(x):
    return jnp.log1p(jnp.exp(-jnp.abs(x)))


def _shift_rows(cur, prev, j):
    p = prev.shape[0]
    ext = jnp.concatenate([prev, cur], axis=0)
    return pltpu.roll(ext, j, 0)[p:]


def _split2(x):
    hi = x.astype(BF16)
    lo = (x - hi.astype(F32)).astype(BF16)
    return hi, lo


def _dot(a, b):
    return jnp.dot(a, b, preferred_element_type=F32)


def _dot_tn(a, b):
    return lax.dot_general(a, b, (((0,), (0,)), ((), ())), preferred_element_type=F32)


def _dot_nt(a, b):
    return lax.dot_general(a, b, (((1,), (1,)), ((), ())), preferred_element_type=F32)


def _pair_bcast(x, lane0, rows):
    lane = lax.broadcasted_iota(jnp.int32, (rows, LANE), 1)
    b0 = jnp.broadcast_to(x[:, lane0:lane0 + 1], (rows, LANE))
    b1 = jnp.broadcast_to(x[:, lane0 + 1:lane0 + 2], (rows, LANE))
    return jnp.where(lane < SSD_HEAD_DIM, b0, b1)


def _start_in_proj(x_ref, w_in_ref, vec_ref, h_ref, proj_ref):
    x = x_ref[...]
    ms = jnp.mean(x * x, axis=-1, keepdims=True)
    h_ref[...] = (x * lax.rsqrt(ms + EPS) * vec_ref[R_NORM:R_NORM + 1, :]).astype(BF16)
    starts = list(range(0, D_PROJ_PAD, PROJ_BLOCK))
    done = [0]

    def emit(n):
        for c0 in starts[done[0]:done[0] + n]:
            c1 = min(c0 + PROJ_BLOCK, D_PROJ_PAD)
            proj_ref[:, c0:c1] = _dot(h_ref[...], w_in_ref[:, c0:c1])
        done[0] = min(done[0] + n, len(starts))

    return emit


def _in_proj(x_ref, w_in_ref, vec_ref, h_ref, proj_ref):
    _start_in_proj(x_ref, w_in_ref, vec_ref, h_ref, proj_ref)(N_PROJ_BLOCKS)


def _suffix_operator(t):
    r = lax.broadcasted_iota(jnp.int32, (t, t), 0)
    c = lax.broadcasted_iota(jnp.int32, (t, t), 1)
    same_chunk = jnp.right_shift(r, 6) == jnp.right_shift(c, 6)
    return jnp.where(same_chunk & (c > r), 1.0, 0.0).astype(BF16)


def _head_mask():
    vr = jnp.right_shift(lax.broadcasted_iota(jnp.int32, (GLA_DV, GLA_DK), 0), 6)
    kr = jnp.right_shift(lax.broadcasted_iota(jnp.int32, (GLA_DV, GLA_DK), 1), 5)
    return vr == kr


def _mix_and_out(proj_ref, x_ref, out_ref, seq_tile, emit, head_mask, w_out_ref, suffix_ref,
                 pool_w_ref, vec_ref, hist_a, hist_x, hist_u, gla_st, ssd_st, *, final_norm):
    t = x_ref.shape[0]
    assert t == TILE
    n_chunks = t // CHUNK

    emit(EMIT_PLAN[0])
    misc = proj_ref[:, C_MISC:C_MISC + LANE]
    lane = lax.broadcasted_iota(jnp.int32, (1, LANE), 1)
    dt_mask = (lane >= DT_LANE) & (lane < DT_LANE + SSD_HEADS)
    pre = proj_ref[:, C_GATE:C_GATE + LANE] + vec_ref[R_MISC:R_MISC + 1, 0:LANE]
    log_a = (jnp.minimum(pre, 0.0) - _log1pexp_neg_abs(pre)) * (1.0 / GLA_TAU)
    dt_pre = misc + vec_ref[R_MISC:R_MISC + 1, LANE:2 * LANE]
    dt = jnp.maximum(dt_pre, 0.0) + _log1pexp_neg_abs(dt_pre)
    a_vec = jnp.where(dt_mask, -jnp.exp(vec_ref[R_MISC:R_MISC + 1, 2 * LANE:3 * LANE]), 0.0)
    d_a = jnp.where(dt_mask, dt * a_vec, 0.0)
    decay_in = jnp.concatenate([log_a, d_a], axis=1)
    parts = _split2(decay_in)

    emit(EMIT_PLAN[1])
    suffix_op = suffix_ref[...]
    rev = _dot(suffix_op, parts[0]) + _dot(suffix_op, parts[1])

    xbc_raw = proj_ref[:, C_XBC:C_XBC + SSD_XBC]
    prev_x = hist_x[...]
    xc = xbc_raw * vec_ref[R_SCW + 3:R_SCW + 4, 0:SSD_XBC] + vec_ref[R_SCB:R_SCB + 1, 0:SSD_XBC]
    for j in (1, 2, 3):
        xc = xc + _shift_rows(xbc_raw, prev_x, j) * vec_ref[R_SCW + 3 - j:R_SCW + 4 - j, 0:SSD_XBC]
    hist_x[...] = xbc_raw[t - SUBLANE:, :]
    xbc = _silu(xc)
    xs = xbc[:, 0:BRANCH]
    bm = xbc[:, BRANCH:BRANCH + SSD_GROUPS * SSD_STATE].astype(BF16)
    cm = xbc[:, BRANCH + SSD_GROUPS * SSD_STATE:].astype(BF16)

    q = (proj_ref[:, C_Q:C_Q + GLA_DK] * ((GLA_DK // GLA_HEADS) ** -0.5)).astype(BF16)
    v = proj_ref[:, C_V:C_V + GLA_DV].astype(BF16)
    k_dec = (proj_ref[:, C_K:C_K + GLA_DK] * jnp.exp(rev[:, 0:GLA_DK])).astype(BF16)
    w_tok = jnp.exp(rev[:, LANE:2 * LANE]) * dt
    wx = jnp.concatenate(
        [xs[:, g * LANE:(g + 1) * LANE] * _pair_bcast(w_tok, DT_LANE + 2 * g, t)
         for g in range(SSD_GROUPS)], axis=1).astype(BF16)

    emit(EMIT_PLAN[2])
    def pair_rhs(a, r0):
        z = jnp.zeros((CHUNK, LANE), BF16)
        return jnp.concatenate([jnp.concatenate([a[r0:r0 + CHUNK], z], axis=1),
                                jnp.concatenate([z, a[r0 + CHUNK:r0 + 2 * CHUNK]], axis=1)], axis=0)

    g_upd, s_upd, g_dec, s_dec = [], [], [], []
    for r0 in range(0, t, 2 * CHUNK):
        pr = slice(r0, r0 + 2 * CHUNK)
        gu = _dot_tn(v[pr, :], pair_rhs(k_dec, r0))
        su = [_dot_tn(bm[pr, g * LANE:(g + 1) * LANE], pair_rhs(wx[:, g * LANE:(g + 1) * LANE], r0))
              for g in range(SSD_GROUPS)]
        for half in range(2):
            hs = slice(half * LANE, (half + 1) * LANE)
            g_upd.append(gu[:, hs])
            s_upd.append([su[g][:, hs] for g in range(SSD_GROUPS)])
    for ci in range(n_chunks):
        r0 = ci * CHUNK
        tot = rev[r0:r0 + 1, :] + decay_in[r0:r0 + 1, :]
        g_dec.append(jnp.exp(tot[:, 0:GLA_DK]))
        dec = jnp.exp(tot[:, LANE:2 * LANE])
        s_dec.append([_pair_bcast(dec, DT_LANE + 2 * g, 1) for g in range(SSD_GROUPS)])

    c = proj_ref[:, C_AC:C_AC + BRANCH] * proj_ref[:, C_AH:C_AH + BRANCH]
    prev_a = hist_a[...]
    conv = c * vec_ref[R_CAW + 2:R_CAW + 3, 0:BRANCH]
    conv = conv + _shift_rows(c, prev_a, 1) * vec_ref[R_CAW + 1:R_CAW + 2, 0:BRANCH]
    conv = conv + _shift_rows(c, prev_a, 2) * vec_ref[R_CAW:R_CAW + 1, 0:BRANCH]
    hist_a[...] = c[t - SUBLANE:, :]
    y_a = proj_ref[:, C_AB:C_AB + BRANCH] * conv * _silu(proj_ref[:, C_AZ:C_AZ + BRANCH])

    u = proj_ref[:, C_U:C_U + BRANCH]
    hp = hist_u.shape[0]
    lane_u = lax.broadcasted_iota(jnp.int32, (1, BRANCH), 1)
    grp = jnp.right_shift(lane_u, 6)
    s = jnp.concatenate([hist_u[...], u], axis=0)
    s = s + pltpu.roll(s, 1, 0)
    s = s + jnp.where(grp >= 1, pltpu.roll(s, 2, 0), 0.0)
    s = s + jnp.where(grp >= 2, pltpu.roll(s, 4, 0), 0.0)
    s = s + jnp.where(grp >= 3, pltpu.roll(s, 8, 0), 0.0)
    win = s[hp:, :]
    hist_u[...] = u[t - hp:, :]
    window = jnp.left_shift(2, grp)
    inv_w = 1.0 / window.astype(F32)
    pos1 = lax.broadcasted_iota(jnp.int32, (hp, BRANCH), 0) + 1
    inv_head = jnp.where(seq_tile == 0, 1.0 / jnp.minimum(pos1, window).astype(F32), inv_w)
    pooled = jnp.concatenate([win[0:hp] * inv_head, win[hp:] * inv_w], axis=0) - u
    mixed = _dot(pooled.astype(BF16), pool_w_ref[...])
    y_c = vec_ref[R_B256:R_B256 + 1, 256:512] * mixed * _silu(proj_ref[:, C_PZ:C_PZ + BRANCH])

    emit(EMIT_PLAN[3])
    g_state = gla_st[...]
    s_state = [ssd_st[g] for g in range(SSD_GROUPS)]
    o_parts, y_parts = [], []
    zero_blk = jnp.zeros((SSD_STATE, LANE), BF16)
    for ci in range(n_chunks):
        sl = slice(ci * CHUNK, (ci + 1) * CHUNK)
        g_state = g_state * g_dec[ci] + jnp.where(head_mask, g_upd[ci], 0.0)
        o_parts.append(_dot_nt(q[sl, :], g_state.astype(BF16)))
        for g in range(SSD_GROUPS):
            s_state[g] = s_state[g] * s_dec[ci][g] + s_upd[ci][g]
        s_bd = jnp.concatenate(
            [jnp.concatenate([s_state[0].astype(BF16), zero_blk], axis=1),
             jnp.concatenate([zero_blk, s_state[1].astype(BF16)], axis=1)], axis=0)
        y_parts.append(_dot(cm[sl, :], s_bd))
    gla_st[...] = g_state
    for g in range(SSD_GROUPS):
        ssd_st[g] = s_state[g]

    emit(EMIT_PLAN[4])
    o = jnp.concatenate(o_parts, axis=0)
    o2 = o * o
    lane_lo = lax.broadcasted_iota(jnp.int32, (1, LANE), 1) < (GLA_DV // GLA_HEADS)
    ms_tiles = []
    for j in range(GLA_DV // LANE):
        o2_j = o2[:, j * LANE:(j + 1) * LANE]
        s_lo = jnp.sum(jnp.where(lane_lo, o2_j, 0.0), axis=-1, keepdims=True)
        s_hi = jnp.sum(jnp.where(lane_lo, 0.0, o2_j), axis=-1, keepdims=True)
        ms_tiles.append(jnp.where(lane_lo, s_lo, s_hi))
    o_ms = jnp.concatenate(ms_tiles, axis=1) * (1.0 / (GLA_DV // GLA_HEADS))
    y_b = (o * lax.rsqrt(o_ms + EPS) * vec_ref[R_B256:R_B256 + 1, 0:256]
           * _silu(proj_ref[:, C_GZ:C_GZ + BRANCH]))

    y = jnp.concatenate(y_parts, axis=0) + vec_ref[R_B256:R_B256 + 1, 512:768] * xs
    y = y * _silu(proj_ref[:, C_SZ:C_SZ + BRANCH])
    y_ms = jnp.mean(y * y, axis=-1, keepdims=True)
    y_d = y * lax.rsqrt(y_ms + EPS) * vec_ref[R_B256:R_B256 + 1, 768:1024]

    emit(EMIT_PLAN[5])
    mix = jnp.concatenate([y_a, y_b, y_c, y_d], axis=1).astype(BF16)
    out = x_ref[...] + _dot(mix, w_out_ref[...])
    if final_norm:
        oms = jnp.mean(out * out, axis=-1, keepdims=True)
        out = out * lax.rsqrt(oms + EPS) * vec_ref[R_FINAL:R_FINAL + 1, :]
    out_ref[...] = out
    emit(N_PROJ_BLOCKS)


def _layer_kernel(x0_ref, x1_ref, x2_ref, w_in_ref, w_out_ref, suffix_ref, pool_w_ref, vec_ref,
                  o_ref, proj0, proj1, h_buf, hist_a, hist_x, hist_u, gla_st, ssd_st,
                  *, steps_per_seq, final_norm):
    t = x0_ref.shape[0]
    j = pl.program_id(0)
    seq_step = lax.rem(j, steps_per_seq)

    @pl.when(seq_step == 0)
    def _():
        hist_a[...] = jnp.zeros_like(hist_a)
        hist_x[...] = jnp.zeros_like(hist_x)
        hist_u[...] = jnp.zeros_like(hist_u)
        gla_st[...] = jnp.zeros_like(gla_st)
        ssd_st[...] = jnp.zeros_like(ssd_st)

    @pl.when(j == 0)
    def _():
        _in_proj(x0_ref, w_in_ref, vec_ref, h_buf, proj0)

    mix = functools.partial(
        _mix_and_out, head_mask=_head_mask(), w_out_ref=w_out_ref, suffix_ref=suffix_ref,
        pool_w_ref=pool_w_ref, vec_ref=vec_ref, hist_a=hist_a, hist_x=hist_x, hist_u=hist_u,
        gla_st=gla_st, ssd_st=ssd_st, final_norm=final_norm)
    emit = _start_in_proj(x1_ref, w_in_ref, vec_ref, h_buf, proj1)
    mix(proj0, x0_ref, o_ref.at[pl.ds(0, t)], 2 * seq_step, emit)
    emit = _start_in_proj(x2_ref, w_in_ref, vec_ref, h_buf, proj0)
    mix(proj1, x1_ref, o_ref.at[pl.ds(t, t)], 2 * seq_step + 1, emit)


def _layer_call(x2d, w_in, w_out, suffix_op, pool_w, vecs, *, batch, final_norm):
    n_tiles = x2d.shape[0] // TILE
    tiles_per_seq = n_tiles // batch
    assert tiles_per_seq % 2 == 0
    steps = n_tiles // 2
    const = lambda j: (0, 0)
    x_spec = lambda f: pl.BlockSpec((TILE, D_MODEL), f)
    return pl.pallas_call(
        functools.partial(_layer_kernel, steps_per_seq=tiles_per_seq // 2, final_norm=final_norm),
        out_shape=jax.ShapeDtypeStruct(x2d.shape, F32),
        grid=(steps,),
        in_specs=[
            x_spec(lambda j: (2 * j, 0)),
            x_spec(lambda j: (2 * j + 1, 0)),
            x_spec(lambda j: (jnp.minimum(2 * j + 2, n_tiles - 1), 0)),
            pl.BlockSpec(w_in.shape, const),
            pl.BlockSpec(w_out.shape, const),
            pl.BlockSpec(suffix_op.shape, const),
            pl.BlockSpec(pool_w.shape, const),
            pl.BlockSpec(vecs.shape, const),
        ],
        out_specs=pl.BlockSpec((2 * TILE, D_MODEL), lambda j: (j, 0)),
        scratch_shapes=[
            pltpu.VMEM((TILE, D_PROJ_PAD), F32),
            pltpu.VMEM((TILE, D_PROJ_PAD), F32),
            pltpu.VMEM((TILE, D_MODEL), BF16),
            pltpu.VMEM((SUBLANE, BRANCH), F32),
            pltpu.VMEM((SUBLANE, SSD_XBC), F32),
            pltpu.VMEM((2 * SUBLANE, BRANCH), F32),
            pltpu.VMEM((GLA_DV, GLA_DK), F32),
            pltpu.VMEM((SSD_GROUPS, SSD_STATE, LANE), F32),
        ],
        compiler_params=pltpu.CompilerParams(
            dimension_semantics=("arbitrary",),
            vmem_limit_bytes=VMEM_LIMIT_BYTES),
    )(x2d, x2d, x2d, w_in, w_out, suffix_op, pool_w, vecs)


def _w_in_prep_kernel(wt_ref, gate_w_ref, o_ref):
    for c0 in range(0, C_GATE, LANE):
        src = c0 if c0 < C_GZ else c0 + GLA_GATE_RANK
        o_ref[:, c0:c0 + LANE] = wt_ref[src:src + LANE, :].T
    g_lr_t = wt_ref[C_GZ:C_GZ + GLA_GATE_RANK, :]
    o_ref[:, C_GATE:C_GATE + LANE] = _dot_tn(gate_w_ref[...], g_lr_t).astype(BF16).T
    dt0 = D_PROJ_SRC - SSD_HEADS
    d_model = wt_ref.shape[1]
    dt_rows = jnp.concatenate(
        [jnp.zeros((DT_LANE, d_model), F32), wt_ref[dt0:dt0 + SSD_HEADS, :].astype(F32),
         jnp.zeros((LANE - DT_LANE - SSD_HEADS, d_model), F32)], axis=0)
    o_ref[:, C_MISC:C_MISC + LANE] = dt_rows.astype(BF16).T


def _prep_w_in(wt, gate_w):
    return pl.pallas_call(
        _w_in_prep_kernel,
        out_shape=jax.ShapeDtypeStruct((D_MODEL, D_PROJ_PAD), BF16),
        compiler_params=pltpu.CompilerParams(vmem_limit_bytes=VMEM_LIMIT_BYTES),
    )(wt, gate_w)


def _row(vals, width=D_MODEL):
    v = jnp.concatenate([jnp.ravel(p).astype(F32) for p in vals])
    return jnp.pad(v, (0, width - v.shape[0]))[None, :]


def _pack_vecs(layer, norm_w, gla_gate_b, gla_norm_w, pool_scale, conv_a_w, ssd_conv_w, ssd_conv_b,
               ssd_dt_bias, ssd_a_log, ssd_d, ssd_norm_w, final_norm_w):
    lane_pad = jnp.zeros((DT_LANE,), F32)
    tail_pad = jnp.zeros((LANE - DT_LANE - SSD_HEADS,), F32)
    rows = [
        _row([norm_w[layer]]),
        _row([final_norm_w]),
        _row([gla_gate_b[layer], lane_pad, ssd_dt_bias[layer], tail_pad,
              lane_pad, ssd_a_log[layer], tail_pad]),
        _row([jnp.tile(gla_norm_w[layer], GLA_HEADS), pool_scale[layer],
              jnp.repeat(ssd_d[layer], SSD_HEAD_DIM), ssd_norm_w[layer]]),
        _row([ssd_conv_b[layer]]),
    ]
    rows += [_row([conv_a_w[layer, r]]) for r in range(CONV_A_WIDTH)]
    rows += [_row([ssd_conv_w[layer, r]]) for r in range(SSD_CONV)]
    rows.append(jnp.zeros((VEC_ROWS - len(rows), D_MODEL), F32))
    return jnp.concatenate(rows, axis=0)


def _block_diag(blocks):
    n = blocks.shape[0]
    rows = []
    for g in range(n):
        rows.append(jnp.concatenate(
            [blocks[g] if j == g else jnp.zeros_like(blocks[g]) for j in range(n)], axis=1))
    return jnp.concatenate(rows, axis=0)


def kernel(x, norm_w, w_in, conv_a_w, gla_gate_w, gla_gate_b, gla_norm_w, pool_w, pool_scale,
           ssd_conv_w, ssd_conv_b, ssd_dt_bias, ssd_a_log, ssd_d, ssd_norm_w, w_out, final_norm_w):
    batch, seq, d_model = x.shape
    depth = w_in.shape[0]
    assert d_model == D_MODEL and seq % (2 * TILE) == 0 and TILE % CHUNK == 0
    h = x.reshape(batch * seq, d_model)
    w_in_t = jnp.transpose(w_in, (2, 0, 1))
    suffix_op = _suffix_operator(TILE)
    for layer in range(depth):
        vecs = _pack_vecs(layer, norm_w, gla_gate_b, gla_norm_w, pool_scale, conv_a_w, ssd_conv_w,
                          ssd_conv_b, ssd_dt_bias, ssd_a_log, ssd_d, ssd_norm_w, final_norm_w)
        w_in_l = _prep_w_in(w_in_t[:, layer, :].astype(BF16), gla_gate_w[layer].astype(BF16))
        h = _layer_call(h, w_in_l, w_out[layer].astype(BF16), suffix_op,
                        _block_diag(pool_w[layer]).astype(BF16), vecs,
                        batch=batch, final_norm=(layer == depth - 1))
    return h.reshape(batch, seq, d_model)
```

```python
import functools

import jax
import jax.numpy as jnp
from jax import lax
from jax.experimental import pallas as pl
from jax.experimental.pallas import tpu as pltpu

D_MODEL = 1024
CHUNK = 64
BRANCH = 256
EPS = 1e-6
CONV_A_WIDTH = 3
GLA_HEADS = 4
GLA_DK = 128
GLA_DV = 256
GLA_GATE_RANK = 16
GLA_TAU = 16.0
POOL_WINDOWS = (2, 4, 8, 16)
POOL_GROUP = 64
SSD_HEAD_DIM = 64
SSD_HEADS = 4
SSD_GROUPS = 2
SSD_STATE = 128
SSD_CONV = 4
SSD_XBC = 768

C_AH, C_AB, C_AC, C_AZ = 0, 256, 512, 768
C_Q, C_K, C_V, C_GZ = 1024, 1152, 1280, 1536
C_U, C_PZ = 1792, 2048
C_SZ, C_XBC = 2304, 2560
C_GATE = 3328
C_MISC = 3456
D_PROJ_PAD = 3584
DT_LANE = 16
D_PROJ_SRC = 3348

LANE = 128
SUBLANE = 8
TILE = 256
PROJ_BLOCK = 512
N_PROJ_BLOCKS = -(-D_PROJ_PAD // PROJ_BLOCK)
EMIT_PLAN = (1, 1, 2, 1, 1, 1)
VMEM_LIMIT_BYTES = 48 * 1024 * 1024

R_NORM, R_FINAL, R_MISC, R_B256, R_SCB, R_CAW, R_SCW = 0, 1, 2, 3, 4, 5, 8
VEC_ROWS = 16

F32 = jnp.float32
BF16 = jnp.bfloat16


def _silu(x):
    return x * jax.nn.sigmoid(x)


def _log1pexp_neg_abs(x):
    return jnp.log(1.0 + jnp.exp(-jnp.abs(x)))


def _shift_rows(cur, prev, j):
    p = prev.shape[0]
    ext = jnp.concatenate([prev, cur], axis=0)
    return pltpu.roll(ext, j, 0)[p:]


def _split2(x):
    hi = x.astype(BF16)
    lo = (x - hi.astype(F32)).astype(BF16)
    return hi, lo


def _dot(a, b):
    return jnp.dot(a, b, preferred_element_type=F32)


def _dot_tn(a, b):
    return lax.dot_general(a, b, (((0,), (0,)), ((), ())), preferred_element_type=F32)


def _dot_nt(a, b):
    return lax.dot_general(a, b, (((1,), (1,)), ((), ())), preferred_element_type=F32)


def _pair_bcast(x, lane0, rows):
    lane = lax.broadcasted_iota(jnp.int32, (rows, LANE), 1)
    b0 = jnp.broadcast_to(x[:, lane0:lane0 + 1], (rows, LANE))
    b1 = jnp.broadcast_to(x[:, lane0 + 1:lane0 + 2], (rows, LANE))
    return jnp.where(lane < SSD_HEAD_DIM, b0, b1)


def _start_in_proj(x_ref, w_in_ref, vec_ref, h_ref, proj_ref):
    x = x_ref[...]
    ms = jnp.mean(x * x, axis=-1, keepdims=True)
    h_ref[...] = (x * lax.rsqrt(ms + EPS)).astype(BF16)
    starts = list(range(0, D_PROJ_PAD, PROJ_BLOCK))
    done = [0]

    def emit(n):
        for c0 in starts[done[0]:done[0] + n]:
            c1 = min(c0 + PROJ_BLOCK, D_PROJ_PAD)
            proj_ref[:, c0:c1] = _dot(h_ref[...], w_in_ref[:, c0:c1])
        done[0] = min(done[0] + n, len(starts))

    return emit


def _in_proj(x_ref, w_in_ref, vec_ref, h_ref, proj_ref):
    _start_in_proj(x_ref, w_in_ref, vec_ref, h_ref, proj_ref)(N_PROJ_BLOCKS)


def _suffix_operator(t):
    r = lax.broadcasted_iota(jnp.int32, (t, t), 0)
    c = lax.broadcasted_iota(jnp.int32, (t, t), 1)
    same_chunk = jnp.right_shift(r, 6) == jnp.right_shift(c, 6)
    return jnp.where(same_chunk & (c > r), 1.0, 0.0).astype(BF16)


def _head_mask():
    vr = jnp.right_shift(lax.broadcasted_iota(jnp.int32, (GLA_DV, GLA_DK), 0), 6)
    kr = jnp.right_shift(lax.broadcasted_iota(jnp.int32, (GLA_DV, GLA_DK), 1), 5)
    return vr == kr


def _mix_and_out(proj_ref, x_ref, out_ref, seq_tile, emit, head_mask, w_out_ref, suffix_ref,
                 pool_w_ref, vec_ref, hist_a, hist_x, hist_u, gla_st, ssd_st, *, final_norm):
    t = x_ref.shape[0]
    assert t == TILE
    n_chunks = t // CHUNK

    emit(EMIT_PLAN[0])
    misc = proj_ref[:, C_MISC:C_MISC + LANE]
    lane = lax.broadcasted_iota(jnp.int32, (1, LANE), 1)
    dt_mask = (lane >= DT_LANE) & (lane < DT_LANE + SSD_HEADS)
    pre = proj_ref[:, C_GATE:C_GATE + LANE] + vec_ref[R_MISC:R_MISC + 1, 0:LANE]
    log_a = (jnp.minimum(pre, 0.0) - _log1pexp_neg_abs(pre)) * (1.0 / GLA_TAU)
    dt_pre = misc + vec_ref[R_MISC:R_MISC + 1, LANE:2 * LANE]
    dt = jnp.maximum(dt_pre, 0.0) + _log1pexp_neg_abs(dt_pre)
    a_vec = jnp.where(dt_mask, -jnp.exp(vec_ref[R_MISC:R_MISC + 1, 2 * LANE:3 * LANE]), 0.0)
    d_a = jnp.where(dt_mask, dt * a_vec, 0.0)
    decay_in = jnp.concatenate([log_a, d_a], axis=1)
    parts = _split2(decay_in)

    emit(EMIT_PLAN[1])
    suffix_op = suffix_ref[...]
    rev = _dot(suffix_op, parts[0]) + _dot(suffix_op, parts[1])

    xbc_raw = proj_ref[:, C_XBC:C_XBC + SSD_XBC]
    prev_x = hist_x[...]
    xc = xbc_raw * vec_ref[R_SCW + 3:R_SCW + 4, 0:SSD_XBC] + vec_ref[R_SCB:R_SCB + 1, 0:SSD_XBC]
    for j in (1, 2, 3):
        xc = xc + _shift_rows(xbc_raw, prev_x, j) * vec_ref[R_SCW + 3 - j:R_SCW + 4 - j, 0:SSD_XBC]
    hist_x[...] = xbc_raw[t - SUBLANE:, :]
    xbc = _silu(xc)
    xs = xbc[:, 0:BRANCH]
    bm = xbc[:, BRANCH:BRANCH + SSD_GROUPS * SSD_STATE].astype(BF16)
    cm = xbc[:, BRANCH + SSD_GROUPS * SSD_STATE:].astype(BF16)

    q = proj_ref[:, C_Q:C_Q + GLA_DK].astype(BF16)
    v = proj_ref[:, C_V:C_V + GLA_DV].astype(BF16)
    k_dec = (proj_ref[:, C_K:C_K + GLA_DK] * jnp.exp(rev[:, 0:GLA_DK])).astype(BF16)
    w_tok = jnp.exp(rev[:, LANE:2 * LANE]) * dt
    wx = jnp.concatenate(
        [xs[:, g * LANE:(g + 1) * LANE] * _pair_bcast(w_tok, DT_LANE + 2 * g, t)
         for g in range(SSD_GROUPS)], axis=1).astype(BF16)

    emit(EMIT_PLAN[2])
    def pair_rhs(a, r0):
        z = jnp.zeros((CHUNK, LANE), BF16)
        return jnp.concatenate([jnp.concatenate([a[r0:r0 + CHUNK], z], axis=1),
                                jnp.concatenate([z, a[r0 + CHUNK:r0 + 2 * CHUNK]], axis=1)], axis=0)

    g_upd, s_upd, g_dec, s_dec = [], [], [], []
    for r0 in range(0, t, 2 * CHUNK):
        pr = slice(r0, r0 + 2 * CHUNK)
        gu = _dot_tn(v[pr, :], pair_rhs(k_dec, r0))
        su = [_dot_tn(bm[pr, g * LANE:(g + 1) * LANE], pair_rhs(wx[:, g * LANE:(g + 1) * LANE], r0))
              for g in range(SSD_GROUPS)]
        for half in range(2):
            hs = slice(half * LANE, (half + 1) * LANE)
            g_upd.append(gu[:, hs])
            s_upd.append([su[g][:, hs] for g in range(SSD_GROUPS)])
    for ci in range(n_chunks):
        r0 = ci * CHUNK
        tot = rev[r0:r0 + 1, :] + decay_in[r0:r0 + 1, :]
        g_dec.append(jnp.exp(tot[:, 0:GLA_DK]))
        dec = jnp.exp(tot[:, LANE:2 * LANE])
        s_dec.append([_pair_bcast(dec, DT_LANE + 2 * g, 1) for g in range(SSD_GROUPS)])

    c = proj_ref[:, C_AC:C_AC + BRANCH] * proj_ref[:, C_AH:C_AH + BRANCH]
    prev_a = hist_a[...]
    conv = c * vec_ref[R_CAW + 2:R_CAW + 3, 0:BRANCH]
    conv = conv + _shift_rows(c, prev_a, 1) * vec_ref[R_CAW + 1:R_CAW + 2, 0:BRANCH]
    conv = conv + _shift_rows(c, prev_a, 2) * vec_ref[R_CAW:R_CAW + 1, 0:BRANCH]
    hist_a[...] = c[t - SUBLANE:, :]
    y_a = proj_ref[:, C_AB:C_AB + BRANCH] * conv * _silu(proj_ref[:, C_AZ:C_AZ + BRANCH])

    u = proj_ref[:, C_U:C_U + BRANCH]
    hp = hist_u.shape[0]
    lane_u = lax.broadcasted_iota(jnp.int32, (1, BRANCH), 1)
    grp = jnp.right_shift(lane_u, 6)
    s = jnp.concatenate([hist_u[...], u], axis=0)
    s = s + pltpu.roll(s, 1, 0)
    s = s + jnp.where(grp >= 1, pltpu.roll(s, 2, 0), 0.0)
    s = s + jnp.where(grp >= 2, pltpu.roll(s, 4, 0), 0.0)
    s = s + jnp.where(grp >= 3, pltpu.roll(s, 8, 0), 0.0)
    win = s[hp:, :]
    hist_u[...] = u[t - hp:, :]
    window = jnp.left_shift(2, grp)
    inv_w = 1.0 / window.astype(F32)
    pos1 = lax.broadcasted_iota(jnp.int32, (hp, BRANCH), 0) + 1
    inv_head = jnp.where(seq_tile == 0, 1.0 / jnp.minimum(pos1, window).astype(F32), inv_w)
    pooled = jnp.concatenate([win[0:hp] * inv_head, win[hp:] * inv_w], axis=0) - u
    mixed = _dot(pooled.astype(BF16), pool_w_ref[...])
    y_c = mixed * _silu(proj_ref[:, C_PZ:C_PZ + BRANCH])

    emit(EMIT_PLAN[3])
    g_state = gla_st[...]
    s_state = [ssd_st[g] for g in range(SSD_GROUPS)]
    o_parts, y_parts = [], []
    zero_blk = jnp.zeros((SSD_STATE, LANE), BF16)
    for ci in range(n_chunks):
        sl = slice(ci * CHUNK, (ci + 1) * CHUNK)
        g_state = g_state * g_dec[ci] + jnp.where(head_mask, g_upd[ci], 0.0)
        o_parts.append(_dot_nt(q[sl, :], g_state.astype(BF16)))
        for g in range(SSD_GROUPS):
            s_state[g] = s_state[g] * s_dec[ci][g] + s_upd[ci][g]
        s_bd = jnp.concatenate(
            [jnp.concatenate([s_state[0].astype(BF16), zero_blk], axis=1),
             jnp.concatenate([zero_blk, s_state[1].astype(BF16)], axis=1)], axis=0)
        y_parts.append(_dot(cm[sl, :], s_bd))
    gla_st[...] = g_state
    for g in range(SSD_GROUPS):
        ssd_st[g] = s_state[g]

    emit(EMIT_PLAN[4])
    o = jnp.concatenate(o_parts, axis=0)
    o2 = o * o
    lane_lo = lax.broadcasted_iota(jnp.int32, (1, LANE), 1) < (GLA_DV // GLA_HEADS)
    ms_tiles = []
    for j in range(GLA_DV // LANE):
        o2_j = o2[:, j * LANE:(j + 1) * LANE]
        s_lo = jnp.sum(jnp.where(lane_lo, o2_j, 0.0), axis=-1, keepdims=True)
        s_hi = jnp.sum(jnp.where(lane_lo, 0.0, o2_j), axis=-1, keepdims=True)
        ms_tiles.append(jnp.where(lane_lo, s_lo, s_hi))
    o_ms = jnp.concatenate(ms_tiles, axis=1) * (1.0 / (GLA_DV // GLA_HEADS))
    y_b = o * lax.rsqrt(o_ms + EPS) * _silu(proj_ref[:, C_GZ:C_GZ + BRANCH])

    y = jnp.concatenate(y_parts, axis=0) + vec_ref[R_B256:R_B256 + 1, 512:768] * xs
    y = y * _silu(proj_ref[:, C_SZ:C_SZ + BRANCH])
    y_ms = jnp.mean(y * y, axis=-1, keepdims=True)
    y_d = y * lax.rsqrt(y_ms + EPS)

    emit(EMIT_PLAN[5])
    mix = jnp.concatenate([y_a, y_b, y_c, y_d], axis=1).astype(BF16)
    out = x_ref[...] + _dot(mix, w_out_ref[...])
    if final_norm:
        oms = jnp.mean(out * out, axis=-1, keepdims=True)
        out = out * lax.rsqrt(oms + EPS) * vec_ref[R_FINAL:R_FINAL + 1, :]
    out_ref[...] = out
    emit(N_PROJ_BLOCKS)


def _layer_kernel(x0_ref, x1_ref, x2_ref, w_in_ref, w_out_ref, suffix_ref, pool_w_ref, vec_ref,
                  o_ref, proj0, proj1, h_buf, hist_a, hist_x, hist_u, gla_st, ssd_st,
                  *, steps_per_seq, final_norm):
    t = x0_ref.shape[0]
    j = pl.program_id(0)
    seq_step = lax.rem(j, steps_per_seq)

    @pl.when(seq_step == 0)
    def _():
        hist_a[...] = jnp.zeros_like(hist_a)
        hist_x[...] = jnp.zeros_like(hist_x)
        hist_u[...] = jnp.zeros_like(hist_u)
        gla_st[...] = jnp.zeros_like(gla_st)
        ssd_st[...] = jnp.zeros_like(ssd_st)

    @pl.when(j == 0)
    def _():
        _in_proj(x0_ref, w_in_ref, vec_ref, h_buf, proj0)

    mix = functools.partial(
        _mix_and_out, head_mask=_head_mask(), w_out_ref=w_out_ref, suffix_ref=suffix_ref,
        pool_w_ref=pool_w_ref, vec_ref=vec_ref, hist_a=hist_a, hist_x=hist_x, hist_u=hist_u,
        gla_st=gla_st, ssd_st=ssd_st, final_norm=final_norm)
    emit = _start_in_proj(x1_ref, w_in_ref, vec_ref, h_buf, proj1)
    mix(proj0, x0_ref, o_ref.at[pl.ds(0, t)], 2 * seq_step, emit)
    emit = _start_in_proj(x2_ref, w_in_ref, vec_ref, h_buf, proj0)
    mix(proj1, x1_ref, o_ref.at[pl.ds(t, t)], 2 * seq_step + 1, emit)


def _layer_call(x2d, w_in, w_out, suffix_op, pool_w, vecs, *, layer, batch, final_norm):
    n_tiles = x2d.shape[0] // TILE
    tiles_per_seq = n_tiles // batch
    assert tiles_per_seq % 2 == 0
    steps = n_tiles // 2
    const = lambda j: (0, 0)
    x_spec = lambda f: pl.BlockSpec((TILE, D_MODEL), f)
    return pl.pallas_call(
        functools.partial(_layer_kernel, steps_per_seq=tiles_per_seq // 2, final_norm=final_norm),
        out_shape=jax.ShapeDtypeStruct(x2d.shape, F32),
        grid=(steps,),
        in_specs=[
            x_spec(lambda j: (2 * j, 0)),
            x_spec(lambda j: (2 * j + 1, 0)),
            x_spec(lambda j: (jnp.minimum(2 * j + 2, n_tiles - 1), 0)),
            pl.BlockSpec((None,) + w_in.shape[1:], lambda j: (layer, 0, 0)),
            pl.BlockSpec(w_out.shape, const),
            pl.BlockSpec(suffix_op.shape, const),
            pl.BlockSpec(pool_w.shape, const),
            pl.BlockSpec(vecs.shape, const),
        ],
        out_specs=pl.BlockSpec((2 * TILE, D_MODEL), lambda j: (j, 0)),
        scratch_shapes=[
            pltpu.VMEM((TILE, D_PROJ_PAD), F32),
            pltpu.VMEM((TILE, D_PROJ_PAD), F32),
            pltpu.VMEM((TILE, D_MODEL), BF16),
            pltpu.VMEM((SUBLANE, BRANCH), F32),
            pltpu.VMEM((SUBLANE, SSD_XBC), F32),
            pltpu.VMEM((2 * SUBLANE, BRANCH), F32),
            pltpu.VMEM((GLA_DV, GLA_DK), F32),
            pltpu.VMEM((SSD_GROUPS, SSD_STATE, LANE), F32),
        ],
        compiler_params=pltpu.CompilerParams(
            dimension_semantics=("arbitrary",),
            vmem_limit_bytes=VMEM_LIMIT_BYTES),
    )(x2d, x2d, x2d, w_in, w_out, suffix_op, pool_w, vecs)


def _w_in_prep_kernel(wt_ref, gate_w_ref, gain_ref, o_ref):
    gain = gain_ref[...]

    def put(c0, rows_f32):
        o_ref[:, c0:c0 + rows_f32.shape[0]] = (rows_f32 * gain).astype(BF16).T

    for c0 in range(0, C_GATE, LANE):
        src = c0 if c0 < C_GZ else c0 + GLA_GATE_RANK
        rows = wt_ref[src:src + LANE, :].astype(F32)
        if c0 == C_Q:
            rows = rows * ((GLA_DK // GLA_HEADS) ** -0.5)
        put(c0, rows)
    g_lr_t = wt_ref[C_GZ:C_GZ + GLA_GATE_RANK, :]
    put(C_GATE, _dot_tn(gate_w_ref[...].astype(BF16), g_lr_t.astype(BF16)))
    dt0 = D_PROJ_SRC - SSD_HEADS
    d_model = wt_ref.shape[1]
    put(C_MISC, jnp.concatenate(
        [jnp.zeros((DT_LANE, d_model), F32), wt_ref[dt0:dt0 + SSD_HEADS, :].astype(F32),
         jnp.zeros((LANE - DT_LANE - SSD_HEADS, d_model), F32)], axis=0))


def _prep_w_in(wt, gate_w, gain):
    depth, n_src, d_model = wt.shape
    per_layer = lambda shape: pl.BlockSpec((None,) + shape, lambda l: (l, 0, 0))
    return pl.pallas_call(
        _w_in_prep_kernel,
        out_shape=jax.ShapeDtypeStruct((depth, d_model, D_PROJ_PAD), BF16),
        grid=(depth,),
        in_specs=[per_layer((n_src, d_model)), per_layer(gate_w.shape[1:]), per_layer((1, d_model))],
        out_specs=per_layer((d_model, D_PROJ_PAD)),
        compiler_params=pltpu.CompilerParams(
            dimension_semantics=("arbitrary",), vmem_limit_bytes=VMEM_LIMIT_BYTES),
    )(wt, gate_w, gain)


def _row(vals, width=D_MODEL):
    v = jnp.concatenate([jnp.ravel(p).astype(F32) for p in vals])
    return jnp.pad(v, (0, width - v.shape[0]))[None, :]


def _pack_vecs(layer, norm_w, gla_gate_b, gla_norm_w, pool_scale, conv_a_w, ssd_conv_w, ssd_conv_b,
               ssd_dt_bias, ssd_a_log, ssd_d, ssd_norm_w, final_norm_w):
    lane_pad = jnp.zeros((DT_LANE,), F32)
    tail_pad = jnp.zeros((LANE - DT_LANE - SSD_HEADS,), F32)
    rows = [
        _row([norm_w[layer]]),
        _row([final_norm_w]),
        _row([gla_gate_b[layer], lane_pad, ssd_dt_bias[layer], tail_pad,
              lane_pad, ssd_a_log[layer], tail_pad]),
        _row([jnp.tile(gla_norm_w[layer], GLA_HEADS), pool_scale[layer],
              jnp.repeat(ssd_d[layer], SSD_HEAD_DIM), ssd_norm_w[layer]]),
        _row([ssd_conv_b[layer]]),
    ]
    rows += [_row([conv_a_w[layer, r]]) for r in range(CONV_A_WIDTH)]
    rows += [_row([ssd_conv_w[layer, r]]) for r in range(SSD_CONV)]
    rows.append(jnp.zeros((VEC_ROWS - len(rows), D_MODEL), F32))
    return jnp.concatenate(rows, axis=0)


def _block_diag(blocks):
    n = blocks.shape[0]
    rows = []
    for g in range(n):
        rows.append(jnp.concatenate(
            [blocks[g] if j == g else jnp.zeros_like(blocks[g]) for j in range(n)], axis=1))
    return jnp.concatenate(rows, axis=0)


def kernel(x, norm_w, w_in, conv_a_w, gla_gate_w, gla_gate_b, gla_norm_w, pool_w, pool_scale,
           ssd_conv_w, ssd_conv_b, ssd_dt_bias, ssd_a_log, ssd_d, ssd_norm_w, w_out, final_norm_w):
    batch, seq, d_model = x.shape
    depth = w_in.shape[0]
    assert d_model == D_MODEL and seq % (2 * TILE) == 0 and TILE % CHUNK == 0
    h = x.reshape(batch * seq, d_model)
    w_in_all = _prep_w_in(jnp.transpose(w_in, (0, 2, 1)).astype(BF16), gla_gate_w, norm_w[:, None, :])
    suffix_op = _suffix_operator(TILE)
    for layer in range(depth):
        vecs = _pack_vecs(layer, norm_w, gla_gate_b, gla_norm_w, pool_scale, conv_a_w, ssd_conv_w,
                          ssd_conv_b, ssd_dt_bias, ssd_a_log, ssd_d, ssd_norm_w, final_norm_w)
        row_gain = jnp.concatenate([jnp.ones((BRANCH,), F32), jnp.tile(gla_norm_w[layer], GLA_HEADS),
                                    jnp.ones((BRANCH,), F32), ssd_norm_w[layer]])
        w_out_l = (w_out[layer] * row_gain[:, None]).astype(BF16)
        pool_l = (_block_diag(pool_w[layer]) * pool_scale[layer][None, :]).astype(BF16)
        h = _layer_call(h, w_in_all, w_out_l, suffix_op, pool_l, vecs,
                        layer=layer, batch=batch, final_norm=(layer == depth - 1))
    return h.reshape(batch, seq, d_model)
```
